```python
import math
import jax, jax.numpy as jnp
from jax import lax
import numpy as np

D_MODEL = 1024
BATCH = 4
SEQ = 4096
DEPTH = 4
DEC_BATCH = 8
DEC_SEQ = 2048
PAST_LEN = 128

N_META = 16
D_MIX = D_MODEL
A_WIDTH = 3 * D_MIX // 8
A_HEADS = 6
A_BLOCK = A_WIDTH // A_HEADS
CONV_WIDTH = 4
CONV_LEFT = 2
RG_C = 8.0
B_WIDTH = 3 * D_MIX // 8
B_HEADS = 6
B_HEAD_DIM = B_WIDTH // B_HEADS
HGRN_CHUNK = 64
C_WIDTH = D_MIX - A_WIDTH - B_WIDTH
C_GROUP = 16
C_GROUPS = C_WIDTH // C_GROUP
C_STATE = 64
D_FF = 4 * D_MODEL
EPS = 1e-6
IN_WIDTHS = (A_WIDTH, A_WIDTH, B_WIDTH, B_WIDTH, B_WIDTH, B_WIDTH, B_WIDTH, C_WIDTH)
D_IN = 2 * A_WIDTH + 5 * B_WIDTH + C_WIDTH

kernel_name = "hybrid_bidir_rglru_hgrn2_s5_encoder"


def rms_norm(x, gain=None):
    x32 = x.astype(jnp.float32)
    y = x32 * lax.rsqrt(jnp.mean(x32 * x32, axis=-1, keepdims=True) + EPS)
    if gain is not None:
        y = y * gain.astype(jnp.float32)
    return y.astype(x.dtype)


def linear_scan(a, b, reverse):
    def combine(left, right):
        a_l, b_l = left
        a_r, b_r = right
        return a_l * a_r, a_r * b_l + b_r
    _, h = lax.associative_scan(combine, (a, b), axis=1, reverse=reverse)
    return h


def rglru_branch(xa, ga, conv_w, conv_b, wr, br, wi, bi, lam):
    Bn, T, _ = xa.shape
    xp = jnp.pad(xa.astype(jnp.float32), ((0, 0), (CONV_LEFT, CONV_WIDTH - 1 - CONV_LEFT), (0, 0)))
    xc = conv_b.astype(jnp.float32)
    for j in range(CONV_WIDTH):
        xc = xc + xp[:, j:j + T] * conv_w[j].astype(jnp.float32)
    xh = xc.reshape(Bn, T, A_HEADS, A_BLOCK)
    h_sum = jnp.zeros_like(xc)
    for d in range(2):
        r = jax.nn.sigmoid(jnp.einsum('btni,nij->btnj', xh, wr[d].astype(jnp.float32)).reshape(Bn, T, A_WIDTH) + br[d].astype(jnp.float32))
        i = jax.nn.sigmoid(jnp.einsum('btni,nij->btnj', xh, wi[d].astype(jnp.float32)).reshape(Bn, T, A_WIDTH) + bi[d].astype(jnp.float32))
        log_a = -RG_C * r * jax.nn.softplus(-lam[d].astype(jnp.float32))
        a = jnp.exp(log_a)
        b = jnp.sqrt(-jnp.expm1(2.0 * log_a)) * (i * xc)
        h_sum = h_sum + linear_scan(a, b, reverse=(d == 1))
    y = h_sum * jax.nn.gelu(ga.astype(jnp.float32))
    return y.astype(xa.dtype)


def hgrn2_chunk_scan(q, k, v, log_f):
    Bn, T, H, dk = q.shape
    dv = v.shape[-1]
    n_chunks = T // HGRN_CHUNK

    def to_chunks(z):
        return z.reshape(Bn, n_chunks, HGRN_CHUNK, H, z.shape[-1]).transpose(1, 0, 3, 2, 4)

    mask = jnp.tril(jnp.ones((HGRN_CHUNK, HGRN_CHUNK), dtype=bool))[:, :, None]

    def step(S, inp):
        qb, kb, vb, gb = inp
        b = jnp.cumsum(gb, axis=2)
        o_inter = jnp.einsum('bhck,bhkv->bhcv', qb * jnp.exp(b), S)
        rel = b[:, :, :, None, :] - b[:, :, None, :, :]
        decay = jnp.exp(jnp.where(mask, rel, -jnp.inf))
        scores = jnp.einsum('bhtk,bhsk,bhtsk->bhts', qb, kb, decay)
        o_intra = jnp.einsum('bhts,bhsv->bhtv', scores, vb)
        b_last = b[:, :, -1:, :]
        S_new = jnp.exp(b_last[:, :, 0, :])[..., None] * S + jnp.einsum('bhsk,bhsv->bhkv', kb * jnp.exp(b_last - b), vb)
        return S_new, o_inter + o_intra

    S0 = jnp.zeros((Bn, H, dk, dv), jnp.float32)
    _, o = lax.scan(step, S0, (to_chunks(q), to_chunks(k), to_chunks(v), to_chunks(log_f)))
    return o.transpose(1, 0, 3, 2, 4).reshape(Bn, T, H, dv)


def hgrn2_branch(xq, xi, xf_fwd, xf_bwd, xg, lower_bound):
    Bn, T, _ = xq.shape
    pad = (-T) % HGRN_CHUNK

    def heads(z):
        return jnp.pad(z, ((0, 0), (pad, 0), (0, 0))).reshape(Bn, T + pad, B_HEADS, B_HEAD_DIM)

    q = heads(jax.nn.silu(xq.astype(jnp.float32)))
    v = heads(xi.astype(jnp.float32))
    o = jnp.zeros((Bn, T, B_HEADS, B_HEAD_DIM), jnp.float32)
    for d, xf in enumerate((xf_fwd, xf_bwd)):
        lb = lower_bound[d].astype(jnp.float32)
        log_f = jnp.logaddexp(jnp.log(lb), jnp.log1p(-lb) + jax.nn.log_sigmoid(xf.astype(jnp.float32)))
        k = -jnp.expm1(log_f)
        lf, kk = heads(log_f), heads(k)
        if d == 0:
            o_d = hgrn2_chunk_scan(q, kk, v, lf)
        else:
            o_d = jnp.flip(hgrn2_chunk_scan(jnp.flip(q, 1), jnp.flip(kk, 1), jnp.flip(v, 1), jnp.flip(lf, 1)), 1)
        o = o + o_d[:, pad:]
    o = rms_norm(o).reshape(Bn, T, B_WIDTH) * jax.nn.silu(xg.astype(jnp.float32))
    return o.astype(xq.dtype)


def s5_branch(u, a_re, a_im, log_dt, b_re, b_im, c_re, c_im, d_skip, glu_w, glu_b):
    Bn, T, _ = u.shape
    u32 = u.astype(jnp.float32)
    ug = u32.reshape(Bn, T, C_GROUPS, C_GROUP).astype(jnp.complex64)
    y = jnp.zeros((Bn, T, C_GROUPS, C_GROUP), jnp.float32)
    for d in range(2):
        lam = lax.complex(a_re[d].astype(jnp.float32), a_im[d].astype(jnp.float32))
        dt = jnp.exp(log_dt[d].astype(jnp.float32))[:, None]
        a_bar = jnp.exp(lam * dt)
        b_bar = ((a_bar - 1.0) / lam)[..., None] * lax.complex(b_re[d].astype(jnp.float32), b_im[d].astype(jnp.float32))
        bu = jnp.einsum('btgh,gph->btgp', ug, b_bar)
        h = linear_scan(jnp.broadcast_to(a_bar, bu.shape), bu, reverse=(d == 1))
        c = lax.complex(c_re[d].astype(jnp.float32), c_im[d].astype(jnp.float32))
        y = y + jnp.real(jnp.einsum('btgp,ghp->btgh', h, c))
    y = y.reshape(Bn, T, C_WIDTH) + d_skip.astype(jnp.float32) * u32
    z = jax.nn.gelu(y)
    out = z * jax.nn.sigmoid(z @ glu_w.astype(jnp.float32) + glu_b.astype(jnp.float32))
    return out.astype(u.dtype)


def _trunk(x, meta_tokens, norm_mix, w_in, conv_w, conv_b, rg_wr, rg_br, rg_wi, rg_bi, rg_lambda,
           hgrn_lb_logits, s5_a_re, s5_a_im, s5_log_dt, s5_b_re, s5_b_im, s5_c_re, s5_c_im, s5_d,
           s5_glu_w, s5_glu_b, mix_gain, w_out, norm_mlp, w_up, w_down, norm_final):
    Bn = x.shape[0]
    meta = jnp.broadcast_to(meta_tokens.astype(x.dtype)[None], (Bn, N_META, D_MODEL))
    h = jnp.concatenate([meta, x], axis=1)
    lb_c = jnp.cumsum(jax.nn.softmax(hgrn_lb_logits.astype(jnp.float32), axis=1), axis=1)
    lower_bounds = lb_c - lb_c[:, :1]
    split_points = [int(s) for s in np.cumsum(IN_WIDTHS)[:-1]]
    for l in range(DEPTH):
        hn = rms_norm(h, norm_mix[l])
        proj = hn @ w_in[l]
        xa, ga, xq, xi, xff, xfb, xg, xu = jnp.split(proj, split_points, axis=-1)
        ya = rglru_branch(xa, ga, conv_w[l], conv_b[l], rg_wr[l], rg_br[l], rg_wi[l], rg_bi[l], rg_lambda[l])
        yb = hgrn2_branch(xq, xi, xff, xfb, xg, lower_bounds[:, l])
        yc = s5_branch(xu, s5_a_re[l], s5_a_im[l], s5_log_dt[l], s5_b_re[l], s5_b_im[l],
                       s5_c_re[l], s5_c_im[l], s5_d[l], s5_glu_w[l], s5_glu_b[l])
        ym = jnp.concatenate([rms_norm(ya), rms_norm(yb), rms_norm(yc)], axis=-1) * mix_gain[l]
        h = h + (ym @ w_out[l]).astype(h.dtype)
        hn = rms_norm(h, norm_mlp[l])
        h = h + (jnp.square(jax.nn.relu(hn @ w_up[l])) @ w_down[l]).astype(h.dtype)
    return rms_norm(h[:, N_META:], norm_final)


def setup_inputs(seed: int = 0) -> dict:
    key = jax.random.key(seed)
    ks = jax.random.split(key, 29)
    f32 = jnp.float32
    nrm = lambda k, shape, s: jax.random.normal(k, shape, f32) * s
    a0 = jax.random.uniform(ks[11], (DEPTH, 2, A_WIDTH), f32, minval=0.9, maxval=0.999)
    p = a0 ** (1.0 / RG_C)
    rg_lambda = jnp.log(p) - jnp.log1p(-p)
    a_im_base = jnp.broadcast_to(math.pi * jnp.arange(C_STATE, dtype=f32), (DEPTH, 2, C_GROUPS, C_STATE))
    return {
        "x_prompt": nrm(ks[0], (BATCH, SEQ, D_MODEL), 1.0),
        "x_sample": nrm(ks[1], (DEC_BATCH, DEC_SEQ, D_MODEL), 1.0),
        "meta_tokens": nrm(ks[2], (N_META, D_MODEL), 1.0),
        "norm_mix": 1.0 + nrm(ks[3], (DEPTH, D_MODEL), 0.02),
        "w_in": nrm(ks[4], (DEPTH, D_MODEL, D_IN), D_MODEL ** -0.5),
        "conv_w": nrm(ks[5], (DEPTH, CONV_WIDTH, A_WIDTH), 0.5),
        "conv_b": nrm(ks[6], (DEPTH, A_WIDTH), 0.01),
        "rg_wr": nrm(ks[7], (DEPTH, 2, A_HEADS, A_BLOCK, A_BLOCK), A_BLOCK ** -0.5),
        "rg_br": nrm(ks[8], (DEPTH, 2, A_WIDTH), 0.01),
        "rg_wi": nrm(ks[9], (DEPTH, 2, A_HEADS, A_BLOCK, A_BLOCK), A_BLOCK ** -0.5),
        "rg_bi": nrm(ks[10], (DEPTH, 2, A_WIDTH), 0.01),
        "rg_lambda": rg_lambda,
        "hgrn_lb_logits": nrm(ks[12], (2, DEPTH, B_WIDTH), 0.1),
        "s5_a_re": -0.5 + nrm(ks[13], (DEPTH, 2, C_GROUPS, C_STATE), 0.01),
        "s5_a_im": a_im_base + nrm(ks[14], (DEPTH, 2, C_GROUPS, C_STATE), 0.01),
        "s5_log_dt": jax.random.uniform(ks[15], (DEPTH, 2, C_GROUPS), f32, minval=math.log(1e-3), maxval=math.log(1e-1)),
        "s5_b_re": nrm(ks[16], (DEPTH, 2, C_GROUPS, C_STATE, C_GROUP), (2.0 * C_GROUP) ** -0.5),
        "s5_b_im": nrm(ks[17], (DEPTH, 2, C_GROUPS, C_STATE, C_GROUP), (2.0 * C_GROUP) ** -0.5),
        "s5_c_re": nrm(ks[18], (DEPTH, 2, C_GROUPS, C_GROUP, C_STATE), (2.0 * C_STATE) ** -0.5),
        "s5_c_im": nrm(ks[19], (DEPTH, 2, C_GROUPS, C_GROUP, C_STATE), (2.0 * C_STATE) ** -0.5),
        "s5_d": nrm(ks[20], (DEPTH, C_WIDTH), 0.1),
        "s5_glu_w": nrm(ks[21], (DEPTH, C_WIDTH, C_WIDTH), C_WIDTH ** -0.5),
        "s5_glu_b": nrm(ks[22], (DEPTH, C_WIDTH), 0.01),
        "mix_gain": 1.0 + nrm(ks[23], (DEPTH, D_MIX), 0.02),
        "w_out": nrm(ks[24], (DEPTH, D_MIX, D_MODEL), D_MIX ** -0.5),
        "norm_mlp": 1.0 + nrm(ks[25], (DEPTH, D_MODEL), 0.02),
        "w_up": nrm(ks[26], (DEPTH, D_MODEL, D_FF), D_MODEL ** -0.5),
        "w_down": nrm(ks[27], (DEPTH, D_FF, D_MODEL), D_FF ** -0.5),
        "norm_final": 1.0 + nrm(ks[28], (D_MODEL,), 0.02),
    }


def reference(x_prompt, x_sample, meta_tokens, norm_mix, w_in, conv_w, conv_b, rg_wr, rg_br, rg_wi, rg_bi,
              rg_lambda, hgrn_lb_logits, s5_a_re, s5_a_im, s5_log_dt, s5_b_re, s5_b_im, s5_c_re, s5_c_im,
              s5_d, s5_glu_w, s5_glu_b, mix_gain, w_out, norm_mlp, w_up, w_down, norm_final):
    params = (meta_tokens, norm_mix, w_in, conv_w, conv_b, rg_wr, rg_br, rg_wi, rg_bi, rg_lambda,
              hgrn_lb_logits, s5_a_re, s5_a_im, s5_log_dt, s5_b_re, s5_b_im, s5_c_re, s5_c_im, s5_d,
              s5_glu_w, s5_glu_b, mix_gain, w_out, norm_mlp, w_up, w_down, norm_final)
    y_prompt = _trunk(x_prompt, *params)
    y_sample = _trunk(x_sample, *params)
    return (y_prompt, y_sample)
```

```python
import functools

import jax
import jax.numpy as jnp
from jax import lax
from jax.experimental import pallas as pl
from jax.experimental.pallas import tpu as pltpu

F32 = jnp.float32
BF16 = jnp.bfloat16

D_MODEL = 1024
DEPTH = 4
N_META = 16
A_WIDTH = 384
A_HEADS = 6
A_BLOCK = 64
CONV_WIDTH = 4
RG_C = 8.0
B_WIDTH = 384
B_HEADS = 6
B_HEAD_DIM = 64
C_WIDTH = 256
C_GROUP = 16
C_GROUPS = 16
C_STATE = 64
D_FF = 4 * D_MODEL
EPS = 1e-6
COL_A = 2 * A_WIDTH
COL_B = 5 * B_WIDTH
D_IN = COL_A + COL_B + C_WIDTH

LANES = 128
SUBLANES = 8
BLK = 16
MICRO = SUBLANES * BLK
SEQ_TILE_MICRO = 3
SEQ_TILE = SEQ_TILE_MICRO * MICRO
ROW_TILE = 512
FF_CHUNK = 1024
VMEM_LIMIT = 56 * 1024 * 1024


def _params(n_axes, vmem=None):
    return pltpu.CompilerParams(dimension_semantics=("arbitrary",) * n_axes, vmem_limit_bytes=vmem)


def _rms(x, width):
    return x * lax.rsqrt(jnp.sum(x * x, axis=-1, keepdims=True) * (1.0 / width) + EPS)


def _sigmoid(x):
    return 1.0 / (1.0 + jnp.exp(-x))


def _token_index(tile, rows):
    r = lax.broadcasted_iota(jnp.int32, (rows, 1), 0)
    return tile * rows + (r // MICRO) * MICRO + (r % SUBLANES) * BLK + (r % MICRO) // SUBLANES


def _in_proj_kernel(h_ref, g_ref, w_ref, oa_ref, ob_ref, oc_ref):
    x = h_ref[...]
    y = (_rms(x, D_MODEL) * g_ref[...]).astype(BF16)
    oa_ref[...] = jnp.dot(y, w_ref[:, 0:COL_A], preferred_element_type=F32)
    ob_ref[...] = jnp.dot(y, w_ref[:, COL_A:COL_A + COL_B], preferred_element_type=F32)
    oc_ref[...] = jnp.dot(y, w_ref[:, COL_A + COL_B:D_IN], preferred_element_type=F32)


def _in_proj(h, gain, w):
    rows = h.shape[0]
    grid = (rows // ROW_TILE,)
    row = lambda c: pl.BlockSpec((ROW_TILE, c), lambda i: (i, 0))
    full = lambda a: pl.BlockSpec(a.shape, lambda i: (0,) * a.ndim)
    return pl.pallas_call(
        _in_proj_kernel,
        grid=grid,
        in_specs=[row(D_MODEL), full(gain), full(w)],
        out_specs=[row(COL_A), row(COL_B), row(C_WIDTH)],
        out_shape=[jax.ShapeDtypeStruct((rows, c), F32) for c in (COL_A, COL_B, C_WIDTH)],
        compiler_params=_params(1, VMEM_LIMIT),
        name="in_proj",
    )(h, gain, w)


def _out_mlp_kernel(ya_ref, yb_ref, ys_ref, u_ref, h_ref, dskip_ref, gluw_ref, glub_ref, gainc_ref,
                    wout_ref, nmlp_ref, wup_ref, wdown_ref, o_ref):
    y = ys_ref[...] + dskip_ref[...] * u_ref[...]
    z = jax.nn.gelu(y)
    gate = _sigmoid(jnp.dot(z.astype(BF16), gluw_ref[...], preferred_element_type=F32) + glub_ref[...])
    yc = _rms(z * gate, C_WIDTH) * gainc_ref[...]
    mix = jnp.dot(ya_ref[...].astype(BF16), wout_ref[0:A_WIDTH, :], preferred_element_type=F32)
    mix += jnp.dot(yb_ref[...].astype(BF16), wout_ref[A_WIDTH:A_WIDTH + B_WIDTH, :], preferred_element_type=F32)
    mix += jnp.dot(yc.astype(BF16), wout_ref[A_WIDTH + B_WIDTH:D_MODEL, :], preferred_element_type=F32)
    h1 = h_ref[...] + mix
    hn = (_rms(h1, D_MODEL) * nmlp_ref[...]).astype(BF16)
    acc = h1
    for c in range(D_FF // FF_CHUNK):
        up = jnp.dot(hn, wup_ref[:, c * FF_CHUNK:(c + 1) * FF_CHUNK], preferred_element_type=F32)
        act = jnp.square(jnp.maximum(up, 0.0)).astype(BF16)
        acc += jnp.dot(act, wdown_ref[c * FF_CHUNK:(c + 1) * FF_CHUNK, :], preferred_element_type=F32)
    o_ref[...] = acc


def _out_mlp(ya, yb, ys, u, h, dskip, gluw, glub, gainc, wout, nmlp, wup, wdown):
    rows = h.shape[0]
    grid = (rows // ROW_TILE,)
    row = lambda c: pl.BlockSpec((ROW_TILE, c), lambda i: (i, 0))
    full = lambda a: pl.BlockSpec(a.shape, lambda i: (0,) * a.ndim)
    once = lambda a: pl.BlockSpec(a.shape, lambda i: (0,) * a.ndim, pipeline_mode=pl.Buffered(1))
    return pl.pallas_call(
        _out_mlp_kernel,
        grid=grid,
        in_specs=[row(A_WIDTH), row(B_WIDTH), row(C_WIDTH), row(C_WIDTH), row(D_MODEL),
                  full(dskip), full(gluw), full(glub), full(gainc), once(wout), full(nmlp), once(wup), once(wdown)],
        out_specs=row(D_MODEL),
        out_shape=jax.ShapeDtypeStruct((rows, D_MODEL), F32),
        compiler_params=_params(1, VMEM_LIMIT),
        name="out_mlp",
    )(ya, yb, ys, u, h, dskip, gluw, glub, gainc, wout, nmlp, wup, wdown)


def _final_norm_kernel(h_ref, g_ref, o_ref):
    o_ref[...] = _rms(h_ref[...], D_MODEL) * g_ref[...]


def _final_norm(h, gain):
    rows = h.shape[0]
    row = pl.BlockSpec((ROW_TILE, D_MODEL), lambda i: (i, 0))
    return pl.pallas_call(
        _final_norm_kernel,
        grid=(rows // ROW_TILE,),
        in_specs=[row, pl.BlockSpec(gain.shape, lambda i: (0, 0))],
        out_specs=row,
        out_shape=jax.ShapeDtypeStruct((rows, D_MODEL), F32),
        compiler_params=_params(1),
        name="final_norm",
    )(h, gain)


def _rg_kernel(*refs, reverse, seq_len, n_tiles):
    if reverse:
        (xa_ref, prev_ref, next_ref, cw_ref, cb_ref, wg_ref, bg_ref, lam_ref,
         out_ref, blk_a, blk_h, blk_c, carry) = refs
    else:
        (xa_ref, ga_ref, hb_ref, prev_ref, next_ref, cw_ref, cb_ref, wg_ref, bg_ref, lam_ref, gain_ref,
         out_ref, blk_a, blk_h, blk_c, carry) = refs
    W = A_WIDTH
    M = SEQ_TILE_MICRO
    nblk = M * SUBLANES
    i = pl.program_id(1)
    tile = (n_tiles - 1 - i) if reverse else i

    @pl.when(i == 0)
    def _():
        carry[...] = jnp.zeros_like(carry)

    tok = _token_index(tile, SEQ_TILE)
    valid = tok < seq_len
    xa = jnp.where(valid, xa_ref[0], 0.0)
    x4 = xa.reshape(M, BLK, SUBLANES, W)
    xs = [x4[:, j] for j in range(BLK)]

    first_tok = tile * SEQ_TILE
    row_id = lax.broadcasted_iota(jnp.int32, (nblk, W), 0)

    def from_prev_block(src, halo_row, halo_tok):
        flat = src.reshape(nblk, W)
        halo = jnp.where(jnp.logical_and(halo_tok >= 0, halo_tok < seq_len), halo_row, 0.0)
        sh = pltpu.roll(flat, 1, axis=0)
        return jnp.where(row_id == 0, halo, sh).reshape(M, SUBLANES, W)

    def from_next_block(src, halo_row, halo_tok):
        flat = src.reshape(nblk, W)
        halo = jnp.where(halo_tok < seq_len, halo_row, 0.0)
        sh = pltpu.roll(flat, nblk - 1, axis=0)
        return jnp.where(row_id == nblk - 1, halo, sh).reshape(M, SUBLANES, W)

    p14 = from_prev_block(xs[BLK - 2], prev_ref[0, 0, 0, SUBLANES - 1:SUBLANES, :], first_tok - 2)
    p15 = from_prev_block(xs[BLK - 1], prev_ref[0, 0, 1, SUBLANES - 1:SUBLANES, :], first_tok - 1)
    n0 = from_next_block(xs[0], next_ref[0, 0, 0, 0:1, :],
                         jnp.where(tile == n_tiles - 1, seq_len, first_tok + SEQ_TILE))
    ext = [p14, p15] + xs + [n0]
    cw = cw_ref[...]
    xc = [cb_ref[...] + cw[0:1] * ext[j] + cw[1:2] * ext[j + 1] + cw[2:3] * ext[j + 2] + cw[3:4] * ext[j + 3]
          for j in range(BLK)]
    xcf = jnp.stack(xc, axis=1).reshape(SEQ_TILE, W)

    g = jnp.dot(xcf.astype(BF16), wg_ref[...], preferred_element_type=F32) + bg_ref[...]
    r = _sigmoid(g[:, 0:W])
    ig = _sigmoid(g[:, W:2 * W])
    nl = -lam_ref[...]
    softplus = jnp.maximum(nl, 0.0) + jnp.log1p(jnp.exp(-jnp.abs(nl)))
    log_a = (-RG_C) * r * softplus
    a = jnp.exp(log_a)
    th = jnp.tanh(log_a)
    b = jnp.where(valid, jnp.sqrt(-2.0 * th / (1.0 - th)) * (ig * xcf), 0.0)
    a4 = a.reshape(M, BLK, SUBLANES, W)
    b4 = b.reshape(M, BLK, SUBLANES, W)

    order = list(range(BLK - 1, -1, -1)) if reverse else list(range(BLK))
    h_loc = [None] * BLK
    a_cum = [None] * BLK
    h = acc = None
    for j in order:
        if h is None:
            h, acc = b4[:, j], a4[:, j]
        else:
            h = a4[:, j] * h + b4[:, j]
            acc = a4[:, j] * acc
        h_loc[j], a_cum[j] = h, acc

    blk_a[...] = acc.reshape(nblk, W)
    blk_h[...] = h.reshape(nblk, W)
    c = carry[...]
    for k in (range(nblk - 1, -1, -1) if reverse else range(nblk)):
        blk_c[k:k + 1, :] = c
        c = blk_a[k:k + 1, :] * c + blk_h[k:k + 1, :]
    carry[...] = c
    cin = blk_c[...].reshape(M, SUBLANES, W)

    hs = jnp.stack([h_loc[j] + a_cum[j] * cin for j in range(BLK)], axis=1).reshape(SEQ_TILE, W)
    if reverse:
        out_ref[0] = hs
    else:
        y = (hs + hb_ref[0]) * jax.nn.gelu(ga_ref[0])
        out_ref[0] = _rms(y, W) * gain_ref[...]


def _rg_call(pa5, pa3, hb, cw, cb, wg, bg, lam, gain, *, reverse, seq_len):
    B, Tp, _ = pa3.shape
    n_tiles = Tp // SEQ_TILE
    n_micro = Tp // MICRO
    M = SEQ_TILE_MICRO
    t_of = (lambda i: n_tiles - 1 - i) if reverse else (lambda i: i)
    tile3 = lambda col: pl.BlockSpec((1, SEQ_TILE, A_WIDTH), lambda b, i: (b, t_of(i), col))
    prev = pl.BlockSpec((1, 1, 2, SUBLANES, A_WIDTH),
                        lambda b, i: (b, jnp.maximum(t_of(i) * M - 1, 0), BLK // 2 - 1, 0, 0))
    nxt = pl.BlockSpec((1, 1, 1, SUBLANES, A_WIDTH),
                       lambda b, i: (b, jnp.minimum((t_of(i) + 1) * M, n_micro - 1), 0, 0, 0))
    full = lambda a: pl.BlockSpec(a.shape, lambda b, i: (0,) * a.ndim)
    nblk = M * SUBLANES
    scratch = [pltpu.VMEM((nblk, A_WIDTH), F32)] * 3 + [pltpu.VMEM((1, A_WIDTH), F32)]
    if reverse:
        args = (pa3, pa5, pa5, cw, cb, wg, bg, lam)
        specs = [tile3(0), prev, nxt, full(cw), full(cb), full(wg), full(bg), full(lam)]
    else:
        args = (pa3, pa3, hb, pa5, pa5, cw, cb, wg, bg, lam, gain)
        specs = [tile3(0), tile3(1), tile3(0), prev, nxt, full(cw), full(cb), full(wg), full(bg), full(lam),
                 full(gain)]
    return pl.pallas_call(
        functools.partial(_rg_kernel, reverse=reverse, seq_len=seq_len, n_tiles=n_tiles),
        grid=(B, n_tiles),
        in_specs=specs,
        out_specs=tile3(0),
        out_shape=jax.ShapeDtypeStruct((B, Tp, A_WIDTH), F32),
        scratch_shapes=scratch,
        compiler_params=_params(2, VMEM_LIMIT),
        name="rglru_bwd" if reverse else "rglru_fwd",
    )(*args)


def _hg_kernel(*refs, reverse, seq_len, n_tiles):
    if reverse:
        (xq_ref, xi_ref, xf_ref, lbp_ref, ebd_ref, pmask_ref,
         out_ref, qin_s, kout_s, v_s, o_s, dec_s, st_s) = refs
    else:
        (xq_ref, xi_ref, xf_ref, xg_ref, ob_ref, lbp_ref, ebd_ref, pmask_ref, gain_ref,
         out_ref, qin_s, kout_s, v_s, o_s, dec_s, st_s) = refs
    W = B_WIDTH
    M = SEQ_TILE_MICRO
    nblk = M * SUBLANES
    n_pairs = W // LANES
    i = pl.program_id(1)
    tile = (n_tiles - 1 - i) if reverse else i

    @pl.when(i == 0)
    def _():
        st_s[...] = jnp.zeros_like(st_s)

    valid = _token_index(tile, SEQ_TILE) < seq_len
    xq = xq_ref[0]
    q = xq * _sigmoid(xq)
    v = jnp.where(valid, xi_ref[0], 0.0)
    x = xf_ref[0]
    log_lb, log_1mlb, one_mlb = lbp_ref[0:1, :], lbp_ref[1:2, :], lbp_ref[2:3, :]
    l1p = jnp.log1p(jnp.exp(-jnp.abs(x)))
    c = log_1mlb + (jnp.minimum(x, 0.0) - l1p)
    lf = jnp.maximum(log_lb, c) + jnp.log1p(jnp.exp(-jnp.abs(log_lb - c)))
    k = one_mlb * jnp.exp(-jnp.maximum(x, 0.0) - l1p)

    q4 = q.reshape(M, BLK, SUBLANES, W)
    k4 = k.reshape(M, BLK, SUBLANES, W)
    v4 = v.reshape(M, BLK, SUBLANES, W)
    lf4 = lf.reshape(M, BLK, SUBLANES, W)

    order = list(range(BLK - 1, -1, -1)) if reverse else list(range(BLK))
    bs = [None] * BLK
    run = None
    for j in order:
        run = lf4[:, j] if run is None else run + lf4[:, j]
        bs[j] = run
    b_last = run

    def put(slab_ref, j, val):
        for p in range(n_pairs):
            for m in range(M):
                slab_ref[p, m * MICRO + j * SUBLANES:m * MICRO + (j + 1) * SUBLANES, :] = (
                    val[m, :, p * LANES:(p + 1) * LANES])

    for j in range(BLK):
        put(qin_s, j, q4[:, j] * jnp.exp(bs[j]))
        put(kout_s, j, k4[:, j] * jnp.exp(b_last - bs[j]))
        put(v_s, j, v4[:, j])
    dec = jnp.exp(b_last).reshape(nblk, W)
    for p in range(n_pairs):
        dec_s[p] = dec[:, p * LANES:(p + 1) * LANES]

    ebd = ebd_ref[...]
    for t in range(BLK):
        srcs = [s for s in range(BLK) if (s >= t if reverse else s <= t)]
        prods = [(q4[:, t] * k4[:, s] * jnp.exp(bs[t] - bs[s])).reshape(nblk, W) for s in srcs]
        scores = jnp.dot(jnp.concatenate(prods, axis=0).astype(BF16), ebd, preferred_element_type=F32)
        o_t = None
        for idx, s in enumerate(srcs):
            term = scores[idx * nblk:(idx + 1) * nblk, :] * v4[:, s].reshape(nblk, W)
            o_t = term if o_t is None else o_t + term
        put(o_s, t, o_t.reshape(M, SUBLANES, W))

    pmask = pmask_ref[...]

    def block_step(step, _):
        blk = (nblk - 1 - step) if reverse else step
        start = (blk // SUBLANES) * MICRO + blk % SUBLANES
        rows = pl.ds(start, BLK, stride=SUBLANES)
        for p in range(n_pairs):
            qb = qin_s[p, rows, :].astype(BF16)
            kb = kout_s[p, rows, :].astype(BF16)
            vb = v_s[p, rows, :].astype(BF16)
            st = st_s[p]
            o_inter = lax.dot_general(qb, st.astype(BF16), (((1,), (1,)), ((), ())), preferred_element_type=F32)
            o_s[p, rows, :] = o_s[p, rows, :] + o_inter
            upd = lax.dot_general(vb, kb, (((0,), (0,)), ((), ())), preferred_element_type=F32)
            st_s[p] = st * dec_s[p, pl.ds(blk, 1), :] + upd * pmask
        return 0

    lax.fori_loop(0, nblk, block_step, 0)

    o = jnp.concatenate([o_s[p] for p in range(n_pairs)], axis=1)
    if reverse:
        out_ref[0] = o
    else:
        tot = o + ob_ref[0]
        msq = jnp.dot((tot * tot).astype(BF16), ebd, preferred_element_type=F32) * (1.0 / B_HEAD_DIM)
        xg = xg_ref[0]
        y = tot * lax.rsqrt(msq + EPS) * (xg * _sigmoid(xg))
        out_ref[0] = _rms(y, W) * gain_ref[...]


def _hg_call(pb, ob, lbp, ebd, pmask, gain, *, reverse, seq_len):
    B, Tp, _ = pb.shape
    n_tiles = Tp // SEQ_TILE
    t_of = (lambda i: n_tiles - 1 - i) if reverse else (lambda i: i)
    tile3 = lambda col: pl.BlockSpec((1, SEQ_TILE, B_WIDTH), lambda b, i: (b, t_of(i), col))
    full = lambda a: pl.BlockSpec(a.shape, lambda b, i: (0,) * a.ndim)
    n_pairs = B_WIDTH // LANES
    nblk = SEQ_TILE_MICRO * SUBLANES
    slab = pltpu.VMEM((n_pairs, SEQ_TILE, LANES), F32)
    scratch = [slab, slab, slab, slab, pltpu.VMEM((n_pairs, nblk, LANES), F32),
               pltpu.VMEM((n_pairs, LANES, LANES), F32)]
    if reverse:
        args = (pb, pb, pb, lbp, ebd, pmask)
        specs = [tile3(0), tile3(1), tile3(3), full(lbp), full(ebd), full(pmask)]
    else:
        args = (pb, pb, pb, pb, ob, lbp, ebd, pmask, gain)
        specs = [tile3(0), tile3(1), tile3(2), tile3(4), tile3(0), full(lbp), full(ebd), full(pmask), full(gain)]
    return pl.pallas_call(
        functools.partial(_hg_kernel, reverse=reverse, seq_len=seq_len, n_tiles=n_tiles),
        grid=(B, n_tiles),
        in_specs=specs,
        out_specs=tile3(0),
        out_shape=jax.ShapeDtypeStruct((B, Tp, B_WIDTH), F32),
        scratch_shapes=scratch,
        compiler_params=_params(2, VMEM_LIMIT),
        name="hgrn2_bwd" if reverse else "hgrn2_fwd",
    )(*args)


def _s5_kernel(u_ref, tz_ref, r_ref, o_ref, arow_ref, y_ref, z_s, hs_s, *, batch, n_chunks, valid_chunks):
    rows = batch * n_chunks
    chunk = lax.broadcasted_iota(jnp.int32, (rows, 1), 0) // batch
    u = jnp.where(chunk < valid_chunks, u_ref[0], 0.0).astype(BF16)
    y_intra = jnp.dot(u, tz_ref[0], preferred_element_type=F32)
    z_s[...] = jnp.dot(u, r_ref[0], preferred_element_type=F32)
    a_re = arow_ref[0, 0:1, :]
    a_im = arow_ref[0, 1:2, :]
    half = C_STATE

    def step(kf, carry):
        hf, hb = carry
        kb = n_chunks - 1 - kf
        rf = pl.ds(pl.multiple_of(kf * batch, SUBLANES), batch)
        rb = pl.ds(pl.multiple_of(kb * batch, SUBLANES), batch)
        hs_s[rf, 0:LANES] = hf
        hs_s[rb, LANES:2 * LANES] = hb
        hf = a_re[:, 0:LANES] * hf + a_im[:, 0:LANES] * pltpu.roll(hf, half, axis=1) + z_s[rf, 0:LANES]
        hb = (a_re[:, LANES:2 * LANES] * hb + a_im[:, LANES:2 * LANES] * pltpu.roll(hb, half, axis=1)
              + z_s[rb, LANES:2 * LANES])
        return hf, hb

    zero = jnp.zeros((batch, LANES), F32)
    lax.fori_loop(0, n_chunks, step, (zero, zero))
    y_ref[0] = y_intra + jnp.dot(hs_s[...].astype(BF16), o_ref[0], preferred_element_type=F32)


def _s5_call(u, tz, rmat, omat, arow, *, batch, n_chunks, valid_chunks):
    G, rows, width = u.shape
    grp = lambda a: pl.BlockSpec((1,) + a.shape[1:], lambda g: (g,) + (0,) * (a.ndim - 1))
    return pl.pallas_call(
        functools.partial(_s5_kernel, batch=batch, n_chunks=n_chunks, valid_chunks=valid_chunks),
        grid=(G,),
        in_specs=[grp(u), grp(tz), grp(rmat), grp(omat), grp(arow)],
        out_specs=grp(u),
        out_shape=jax.ShapeDtypeStruct(u.shape, F32),
        scratch_shapes=[pltpu.VMEM((rows, width), F32), pltpu.VMEM((rows, width), F32)],
        compiler_params=_params(1, VMEM_LIMIT),
        name="s5_chunked",
    )(u, tz, rmat, omat, arow)


def _s5_weights(a_re, a_im, log_dt, b_re, b_im, c_re, c_im):
    hp = lax.Precision.HIGHEST
    lam = lax.complex(a_re.astype(F32), a_im.astype(F32))
    ld = lam * jnp.exp(log_dt.astype(F32))[..., None]
    a_bar = jnp.exp(ld)
    bb = ((a_bar - 1.0) / lam)[..., None] * lax.complex(b_re.astype(F32), b_im.astype(F32))
    cc = lax.complex(c_re.astype(F32), c_im.astype(F32))
    steps = jnp.arange(BLK + 1, dtype=F32)
    apow = jnp.exp(ld[None] * steps[:, None, None, None])
    kern = jnp.real(jnp.einsum('dgop,ldgp,dgpi->dlgoi', cc, apow[:BLK], bb, precision=hp))
    s_idx = jnp.arange(BLK)[:, None]
    t_idx = jnp.arange(BLK)[None, :]
    lag_f = t_idx - s_idx
    kf = jnp.where((lag_f >= 0)[:, :, None, None, None], kern[0][jnp.clip(lag_f, 0, BLK - 1)], 0.0)
    kb = jnp.where((lag_f <= 0)[:, :, None, None, None], kern[1][jnp.clip(-lag_f, 0, BLK - 1)], 0.0)
    tz = (kf + kb).transpose(2, 0, 4, 1, 3).reshape(C_GROUPS, BLK * C_GROUP, BLK * C_GROUP)
    rf = apow[BLK - 1 - jnp.arange(BLK), 0][:, :, None, :] * bb[0].transpose(0, 2, 1)[None]
    rb = apow[jnp.arange(BLK), 1][:, :, None, :] * bb[1].transpose(0, 2, 1)[None]
    to_rows = lambda z: z.transpose(1, 0, 2, 3).reshape(C_GROUPS, BLK * C_GROUP, C_STATE)
    rmat = jnp.concatenate([to_rows(jnp.real(rf)), to_rows(jnp.imag(rf)),
                            to_rows(jnp.real(rb)), to_rows(jnp.imag(rb))], axis=-1)
    wf = cc[0][None] * apow[1:BLK + 1, 0][:, :, None, :]
    wb = cc[1][None] * apow[BLK - jnp.arange(BLK), 1][:, :, None, :]
    to_cols = lambda z: z.transpose(1, 3, 0, 2).reshape(C_GROUPS, C_STATE, BLK * C_GROUP)
    omat = jnp.concatenate([to_cols(jnp.real(wf)), to_cols(-jnp.imag(wf)),
                            to_cols(jnp.real(wb)), to_cols(-jnp.imag(wb))], axis=1)
    a16 = apow[BLK]
    a_re16 = jnp.concatenate([jnp.real(a16[0])] * 2 + [jnp.real(a16[1])] * 2, axis=-1)
    a_im16 = jnp.concatenate([-jnp.imag(a16[0]), jnp.imag(a16[0]), -jnp.imag(a16[1]), jnp.imag(a16[1])], axis=-1)
    arow = jnp.stack([a_re16, a_im16], axis=1)
    return tz.astype(BF16), rmat.astype(BF16), omat.astype(BF16), arow


def _block_diag(w):
    H, n, _ = w.shape
    eye = jnp.eye(H, dtype=w.dtype)
    return (eye[:, None, :, None] * w[:, :, None, :]).reshape(H * n, H * n)


def _to_blocked(x, padded_len):
    B, T, D = x.shape
    x = jnp.pad(x, ((0, 0), (0, padded_len - T), (0, 0)))
    x = x.reshape(B, padded_len // MICRO, SUBLANES, BLK, D).transpose(0, 1, 3, 2, 4)
    return x.reshape(B, padded_len, D)


def _from_blocked(y):
    B, Tp, D = y.shape
    y = y.reshape(B, Tp // MICRO, BLK, SUBLANES, D).transpose(0, 1, 3, 2, 4)
    return y.reshape(B, Tp, D)


def _trunk(x, P):
    B, L, _ = x.shape
    T = N_META + L
    Tp = -(-T // SEQ_TILE) * SEQ_TILE
    rows = B * Tp
    assert rows % ROW_TILE == 0 and T % BLK == 0
    n_micro = Tp // MICRO
    n_chunks = Tp // BLK
    Bp = -(-B // SUBLANES) * SUBLANES
    meta =jnp.broadcast_to(P["meta_tokens"].astype(x.dtype)[None], (B, N_META, D_MODEL))
    h = _to_blocked(jnp.concatenate([meta, x], axis=1), Tp).reshape(rows, D_MODEL)
    for l in range(DEPTH):
        pa, pb, pc = _in_proj(h, P["norm_mix"][l], P["w_in"][l])
        u = pc.reshape(B, n_micro, BLK, SUBLANES, C_GROUPS, C_GROUP).transpose(4, 1, 3, 0, 2, 5)
        u = jnp.pad(u, ((0, 0), (0, 0), (0, 0), (0, Bp - B), (0, 0), (0, 0)))
        u = u.reshape(C_GROUPS, n_chunks * Bp, BLK * C_GROUP)
        ys = _s5_call(u, *P["s5"][l], batch=Bp, n_chunks=n_chunks, valid_chunks=T // BLK)
        ys = ys.reshape(C_GROUPS, n_micro, SUBLANES, Bp, BLK, C_GROUP)[:, :, :, :B].transpose(3, 1, 4, 2, 0, 5)
        ys = ys.reshape(rows, C_WIDTH)
        pa3 = pa.reshape(B, Tp, COL_A)
        pa5 = pa.reshape(B, n_micro, BLK, SUBLANES, COL_A)
        rg = P["rg"][l]
        hb = _rg_call(pa5, pa3, None, rg["cw"], rg["cb"], rg["wg"][1], rg["bg"][1], rg["lam"][1], None,
                      reverse=True, seq_len=T)
        ya = _rg_call(pa5, pa3, hb, rg["cw"], rg["cb"], rg["wg"][0], rg["bg"][0], rg["lam"][0], P["gain_a"][l],
                      reverse=False, seq_len=T)
        pb3 = pb.reshape(B, Tp, COL_B)
        ob = _hg_call(pb3, None, P["lbp"][l][1], P["ebd"], P["pmask"], None, reverse=True, seq_len=T)
        yb = _hg_call(pb3, ob, P["lbp"][l][0], P["ebd"], P["pmask"], P["gain_b"][l], reverse=False, seq_len=T)
        h = _out_mlp(ya.reshape(rows, A_WIDTH), yb.reshape(rows, B_WIDTH), ys, pc, h,
                     P["s5_d"][l], P["glu_w"][l], P["glu_b"][l], P["gain_c"][l],
                     P["w_out"][l], P["norm_mlp"][l], P["w_up"][l], P["w_down"][l])
    y = _final_norm(h, P["norm_final"])
    return _from_blocked(y.reshape(B, Tp, D_MODEL))[:, N_META:T]


def _prepare(meta_tokens, norm_mix, w_in, conv_w, conv_b, rg_wr, rg_br, rg_wi, rg_bi, rg_lambda,
             hgrn_lb_logits, s5_a_re, s5_a_im, s5_log_dt, s5_b_re, s5_b_im, s5_c_re, s5_c_im, s5_d,
             s5_glu_w, s5_glu_b, mix_gain, w_out, norm_mlp, w_up, w_down, norm_final):
    row = lambda a: a.astype(F32)[:, None, :]
    lb_c = jnp.cumsum(jax.nn.softmax(hgrn_lb_logits.astype(F32), axis=1), axis=1)
    lb = (lb_c - lb_c[:, :1]).transpose(1, 0, 2)
    lbp = jnp.stack([jnp.log(lb), jnp.log1p(-lb), 1.0 - lb], axis=2)
    rg = []
    for l in range(DEPTH):
        wg = jnp.stack([jnp.concatenate([_block_diag(rg_wr[l, d]), _block_diag(rg_wi[l, d])], axis=1)
                        for d in range(2)]).astype(BF16)
        bg = jnp.concatenate([rg_br[l], rg_bi[l]], axis=-1).astype(F32)[:, None, :]
        rg.append(dict(cw=conv_w[l].astype(F32), cb=conv_b[l].astype(F32)[None], wg=wg, bg=bg,
                       lam=rg_lambda[l].astype(F32)[:, None, :]))
    head = jnp.arange(B_WIDTH) // B_HEAD_DIM
    ebd = (head[:, None] == head[None, :]).astype(BF16)
    half = jnp.arange(LANES) // B_HEAD_DIM
    pmask = (half[:, None] == half[None, :]).astype(F32)
    gain = mix_gain.astype(F32)
    return dict(
        meta_tokens=meta_tokens, norm_mix=row(norm_mix), w_in=w_in.astype(BF16), rg=rg, lbp=lbp, ebd=ebd,
        pmask=pmask,
        s5=[_s5_weights(s5_a_re[l], s5_a_im[l], s5_log_dt[l], s5_b_re[l], s5_b_im[l], s5_c_re[l], s5_c_im[l])
            for l in range(DEPTH)],
        s5_d=row(s5_d), glu_w=s5_glu_w.astype(BF16), glu_b=row(s5_glu_b),
        gain_a=row(gain[:, 0:A_WIDTH]), gain_b=row(gain[:, A_WIDTH:A_WIDTH + B_WIDTH]),
        gain_c=row(gain[:, A_WIDTH + B_WIDTH:]),
        w_out=w_out.astype(BF16), norm_mlp=row(norm_mlp), w_up=w_up.astype(BF16), w_down=w_down.astype(BF16),
        norm_final=norm_final.astype(F32)[None],
    )


def kernel(x_prompt, x_sample, meta_tokens, norm_mix, w_in, conv_w, conv_b, rg_wr, rg_br, rg_wi, rg_bi, rg_lambda, hgrn_lb_logits, s5_a_re, s5_a_im, s5_log_dt, s5_b_re, s5_b_im, s5_c_re, s5_c_im, s5_d, s5_glu_w, s5_glu_b, mix_gain, w_out, norm_mlp, w_up, w_down, norm_final):
    P = _prepare(meta_tokens, norm_mix, w_in, conv_w, conv_b, rg_wr, rg_br, rg_wi, rg_bi, rg_lambda,
                 hgrn_lb_logits, s5_a_re, s5_a_im, s5_log_dt, s5_b_re, s5_b_im, s5_c_re, s5_c_im, s5_d,
                 s5_glu_w, s5_glu_b, mix_gain, w_out, norm_mlp, w_up, w_down, norm_final)
    return (_trunk(x_prompt, P), _trunk(x_sample, P))
```

```python
import functools

import jax
import jax.numpy as jnp
from jax import lax
from jax.experimental import pallas as pl
from jax.experimental.pallas import tpu as pltpu

F32 = jnp.float32
BF16 = jnp.bfloat16

D_MODEL = 1024
DEPTH = 4
N_META = 16
A_WIDTH = 384
A_HEADS = 6
A_BLOCK = 64
CONV_WIDTH = 4
RG_C = 8.0
B_WIDTH = 384
B_HEADS = 6
B_HEAD_DIM = 64
C_WIDTH = 256
C_GROUP = 16
C_GROUPS = 16
C_STATE = 64
D_FF = 4 * D_MODEL
EPS = 1e-6
LOG2_E = 1.4426950408889634
COL_A = 2 * A_WIDTH
COL_B = 5 * B_WIDTH
D_IN = COL_A + COL_B + C_WIDTH

LANES = 128
SUBLANES = 8
BLK = 16
MICRO = SUBLANES * BLK
SEQ_TILE_MICRO = 3
SEQ_TILE = SEQ_TILE_MICRO * MICRO
ROW_TILE = 512
FF_CHUNK = 1024
VMEM_LIMIT = 56 * 1024 * 1024


def _params(n_axes, vmem=None):
    return pltpu.CompilerParams(dimension_semantics=("arbitrary",) * n_axes, vmem_limit_bytes=vmem)


def _rms(x, width):
    return x * lax.rsqrt(jnp.sum(x * x, axis=-1, keepdims=True) * (1.0 / width) + EPS)


def _sigmoid(x):
    return 1.0 / (1.0 + jnp.exp(-x))


def _token_index(tile, rows):
    r = lax.broadcasted_iota(jnp.int32, (rows, 1), 0)
    return tile * rows + (r // MICRO) * MICRO + (r % SUBLANES) * BLK + (r % MICRO) // SUBLANES


def _in_proj_kernel(h_ref, g_ref, w_ref, oa_ref, ob_ref, oc_ref):
    x = h_ref[...]
    y = (_rms(x, D_MODEL) * g_ref[...]).astype(BF16)
    oa_ref[...] = jnp.dot(y, w_ref[:, 0:COL_A], preferred_element_type=F32)
    ob_ref[...] = jnp.dot(y, w_ref[:, COL_A:COL_A + COL_B], preferred_element_type=F32)
    oc_ref[...] = jnp.dot(y, w_ref[:, COL_A + COL_B:D_IN], preferred_element_type=F32)


def _in_proj(h, gain, w):
    rows = h.shape[0]
    grid = (rows // ROW_TILE,)
    row = lambda c: pl.BlockSpec((ROW_TILE, c), lambda i: (i, 0))
    full = lambda a: pl.BlockSpec(a.shape, lambda i: (0,) * a.ndim)
    return pl.pallas_call(
        _in_proj_kernel,
        grid=grid,
        in_specs=[row(D_MODEL), full(gain), full(w)],
        out_specs=[row(COL_A), row(COL_B), row(C_WIDTH)],
        out_shape=[jax.ShapeDtypeStruct((rows, c), F32) for c in (COL_A, COL_B, C_WIDTH)],
        compiler_params=_params(1, VMEM_LIMIT),
        name="in_proj",
    )(h, gain, w)


def _out_mlp_kernel(ya_ref, yb_ref, ys_ref, u_ref, h_ref, dskip_ref, gluw_ref, glub_ref, gainc_ref,
                    wout_ref, nmlp_ref, wup_ref, wdown_ref, o_ref):
    y = ys_ref[...] + dskip_ref[...] * u_ref[...]
    z = jax.nn.gelu(y)
    gate = _sigmoid(jnp.dot(z.astype(BF16), gluw_ref[...], preferred_element_type=F32) + glub_ref[...])
    yc = _rms(z * gate, C_WIDTH) * gainc_ref[...]
    mix = jnp.dot(ya_ref[...].astype(BF16), wout_ref[0:A_WIDTH, :], preferred_element_type=F32)
    mix += jnp.dot(yb_ref[...].astype(BF16), wout_ref[A_WIDTH:A_WIDTH + B_WIDTH, :], preferred_element_type=F32)
    mix += jnp.dot(yc.astype(BF16), wout_ref[A_WIDTH + B_WIDTH:D_MODEL, :], preferred_element_type=F32)
    h1 = h_ref[...] + mix
    hn = (_rms(h1, D_MODEL) * nmlp_ref[...]).astype(BF16)
    acc = h1
    for c in range(D_FF // FF_CHUNK):
        up = jnp.dot(hn, wup_ref[:, c * FF_CHUNK:(c + 1) * FF_CHUNK], preferred_element_type=F32)
        act = jnp.square(jnp.maximum(up, 0.0)).astype(BF16)
        acc += jnp.dot(act, wdown_ref[c * FF_CHUNK:(c + 1) * FF_CHUNK, :], preferred_element_type=F32)
    o_ref[...] = acc


def _out_mlp(ya, yb, ys, u, h, dskip, gluw, glub, gainc, wout, nmlp, wup, wdown):
    rows = h.shape[0]
    grid = (rows // ROW_TILE,)
    row = lambda c: pl.BlockSpec((ROW_TILE, c), lambda i: (i, 0))
    full = lambda a: pl.BlockSpec(a.shape, lambda i: (0,) * a.ndim)
    once = lambda a: pl.BlockSpec(a.shape, lambda i: (0,) * a.ndim, pipeline_mode=pl.Buffered(1))
    return pl.pallas_call(
        _out_mlp_kernel,
        grid=grid,
        in_specs=[row(A_WIDTH), row(B_WIDTH), row(C_WIDTH), row(C_WIDTH), row(D_MODEL),
                  full(dskip), full(gluw), full(glub), full(gainc), once(wout), full(nmlp), once(wup), once(wdown)],
        out_specs=row(D_MODEL),
        out_shape=jax.ShapeDtypeStruct((rows, D_MODEL), F32),
        compiler_params=_params(1, VMEM_LIMIT),
        name="out_mlp",
    )(ya, yb, ys, u, h, dskip, gluw, glub, gainc, wout, nmlp, wup, wdown)


def _final_norm_kernel(h_ref, g_ref, o_ref):
    o_ref[...] = _rms(h_ref[...], D_MODEL) * g_ref[...]


def _final_norm(h, gain):
    rows = h.shape[0]
    row = pl.BlockSpec((ROW_TILE, D_MODEL), lambda i: (i, 0))
    return pl.pallas_call(
        _final_norm_kernel,
        grid=(rows // ROW_TILE,),
        in_specs=[row, pl.BlockSpec(gain.shape, lambda i: (0, 0))],
        out_specs=row,
        out_shape=jax.ShapeDtypeStruct((rows, D_MODEL), F32),
        compiler_params=_params(1),
        name="final_norm",
    )(h, gain)


def _rg_kernel(*refs, reverse, seq_len, n_tiles):
    if reverse:
        (xa_ref, prev_ref, next_ref, cw_ref, cb_ref, wg_ref, bg_ref, lam_ref,
         out_ref, blk_a, blk_h, blk_c, carry) = refs
    else:
        (xa_ref, ga_ref, hb_ref, prev_ref, next_ref, cw_ref, cb_ref, wg_ref, bg_ref, lam_ref, gain_ref,
         out_ref, blk_a, blk_h, blk_c, carry) = refs
    W = A_WIDTH
    M = SEQ_TILE_MICRO
    nblk = M * SUBLANES
    i = pl.program_id(1)
    tile = (n_tiles - 1 - i) if reverse else i

    @pl.when(i == 0)
    def _():
        carry[...] = jnp.zeros_like(carry)

    tok = _token_index(tile, SEQ_TILE)
    valid = tok < seq_len
    xa = jnp.where(valid, xa_ref[0], 0.0)
    x4 = xa.reshape(M, BLK, SUBLANES, W)
    xs = [x4[:, j] for j in range(BLK)]

    first_tok = tile * SEQ_TILE
    row_id = lax.broadcasted_iota(jnp.int32, (nblk, W), 0)

    def from_prev_block(src, halo_row, halo_tok):
        flat = src.reshape(nblk, W)
        halo = jnp.where(jnp.logical_and(halo_tok >= 0, halo_tok < seq_len), halo_row, 0.0)
        sh = pltpu.roll(flat, 1, axis=0)
        return jnp.where(row_id == 0, halo, sh).reshape(M, SUBLANES, W)

    def from_next_block(src, halo_row, halo_tok):
        flat = src.reshape(nblk, W)
        halo = jnp.where(halo_tok < seq_len, halo_row, 0.0)
        sh = pltpu.roll(flat, nblk - 1, axis=0)
        return jnp.where(row_id == nblk - 1, halo, sh).reshape(M, SUBLANES, W)

    p14 = from_prev_block(xs[BLK - 2], prev_ref[0, SUBLANES - 1:SUBLANES, :], first_tok - 2)
    p15 = from_prev_block(xs[BLK - 1], prev_ref[0, 2 * SUBLANES - 1:2 * SUBLANES, :], first_tok - 1)
    n0 = from_next_block(xs[0], next_ref[0, 0:1, :],
                         jnp.where(tile == n_tiles - 1, seq_len, first_tok + SEQ_TILE))
    ext = [p14, p15] + xs + [n0]
    cw = cw_ref[...]
    xc = [cb_ref[...] + cw[0:1] * ext[j] + cw[1:2] * ext[j + 1] + cw[2:3] * ext[j + 2] + cw[3:4] * ext[j + 3]
          for j in range(BLK)]
    xcf = jnp.stack(xc, axis=1).reshape(SEQ_TILE, W)

    g = jnp.dot(xcf.astype(BF16), wg_ref[...], preferred_element_type=F32) + bg_ref[...]
    r = _sigmoid(g[:, 0:W])
    ig = _sigmoid(g[:, W:2 * W])
    nl = -lam_ref[...]
    softplus = jnp.maximum(nl, 0.0) + jnp.log1p(jnp.exp(-jnp.abs(nl)))
    log_a = (-RG_C) * r * softplus
    a = jnp.exp(log_a)
    th = jnp.tanh(log_a)
    b = jnp.where(valid, jnp.sqrt(-2.0 * th / (1.0 - th)) * (ig * xcf), 0.0)
    a4 = a.reshape(M, BLK, SUBLANES, W)
    b4 = b.reshape(M, BLK, SUBLANES, W)

    order = list(range(BLK - 1, -1, -1)) if reverse else list(range(BLK))
    h_loc = [None] * BLK
    a_cum = [None] * BLK
    h = acc = None
    for j in order:
        if h is None:
            h, acc = b4[:, j], a4[:, j]
        else:
            h = a4[:, j] * h + b4[:, j]
            acc = a4[:, j] * acc
        h_loc[j], a_cum[j] = h, acc

    blk_a[...] = acc.reshape(nblk, W)
    blk_h[...] = h.reshape(nblk, W)
    c = carry[...]
    for k in (range(nblk - 1, -1, -1) if reverse else range(nblk)):
        blk_c[k:k + 1, :] = c
        c = blk_a[k:k + 1, :] * c + blk_h[k:k + 1, :]
    carry[...] = c
    cin = blk_c[...].reshape(M, SUBLANES, W)

    hs = jnp.stack([h_loc[j] + a_cum[j] * cin for j in range(BLK)], axis=1).reshape(SEQ_TILE, W)
    if reverse:
        out_ref[0] = hs
    else:
        y = (hs + hb_ref[0]) * jax.nn.gelu(ga_ref[0])
        out_ref[0] = _rms(y, W) * gain_ref[...]


def _rg_call(pa3, hb, cw, cb, wg, bg, lam, gain, *, reverse, seq_len):
    B, Tp, _ = pa3.shape
    n_tiles = Tp // SEQ_TILE
    M = SEQ_TILE_MICRO
    t_of = (lambda i: n_tiles - 1 - i) if reverse else (lambda i: i)
    tile3 = lambda col: pl.BlockSpec((1, SEQ_TILE, A_WIDTH), lambda b, i: (b, t_of(i), col))
    prev = pl.BlockSpec((1, 2 * SUBLANES, A_WIDTH),
                        lambda b, i: (b, jnp.maximum(t_of(i) * (SEQ_TILE // (2 * SUBLANES)) - 1, 0), 0))
    nxt = pl.BlockSpec((1, SUBLANES, A_WIDTH),
                       lambda b, i: (b, jnp.minimum((t_of(i) + 1) * (SEQ_TILE // SUBLANES), Tp // SUBLANES - 1), 0))
    full = lambda a: pl.BlockSpec(a.shape, lambda b, i: (0,) * a.ndim)
    nblk = M * SUBLANES
    scratch = [pltpu.VMEM((nblk, A_WIDTH), F32)] * 3 + [pltpu.VMEM((1, A_WIDTH), F32)]
    if reverse:
        args = (pa3, pa3, pa3, cw, cb, wg, bg, lam)
        specs = [tile3(0), prev, nxt, full(cw), full(cb), full(wg), full(bg), full(lam)]
    else:
        args = (pa3, pa3, hb, pa3, pa3, cw, cb, wg, bg, lam, gain)
        specs = [tile3(0), tile3(1), tile3(0), prev, nxt, full(cw), full(cb), full(wg), full(bg), full(lam),
                 full(gain)]
    return pl.pallas_call(
        functools.partial(_rg_kernel, reverse=reverse, seq_len=seq_len, n_tiles=n_tiles),
        grid=(B, n_tiles),
        in_specs=specs,
        out_specs=tile3(0),
        out_shape=jax.ShapeDtypeStruct((B, Tp, A_WIDTH), F32),
        scratch_shapes=scratch,
        compiler_params=_params(2, VMEM_LIMIT),
        name="rglru_bwd" if reverse else "rglru_fwd",
    )(*args)


def _hg_kernel(*refs, reverse, seq_len, n_tiles):
    if reverse:
        (xq_ref, xi_ref, xf_ref, lbp_ref, ebd_ref, pmask_ref,
         out_ref, qin_s, kout_s, v_s, o_s, dec_s, st_s, upd_s, ent_s) = refs
    else:
        (xq_ref, xi_ref, xf_ref, xg_ref, ob_ref, lbp_ref, ebd_ref, pmask_ref, gain_ref,
         out_ref, qin_s, kout_s, v_s, o_s, dec_s, st_s, upd_s, ent_s) = refs
    W = B_WIDTH
    M = SEQ_TILE_MICRO
    nblk = M * SUBLANES
    n_pairs = W // LANES
    i = pl.program_id(1)
    tile = (n_tiles - 1 - i) if reverse else i

    @pl.when(i == 0)
    def _():
        st_s[...] = jnp.zeros_like(st_s)

    valid = _token_index(tile, SEQ_TILE) < seq_len
    xq = xq_ref[0]
    q = xq * _sigmoid(xq)
    v = jnp.where(valid, xi_ref[0], 0.0)
    x = xf_ref[0]
    log_lb, log_1mlb, one_mlb = lbp_ref[0:1, :], lbp_ref[1:2, :], lbp_ref[2:3, :]
    l1p = jnp.log1p(jnp.exp(-jnp.abs(x)))
    c = log_1mlb + (jnp.minimum(x, 0.0) - l1p)
    lf = jnp.maximum(log_lb, c) + jnp.log1p(jnp.exp(-jnp.abs(log_lb - c)))
    k = one_mlb * jnp.exp(-jnp.maximum(x, 0.0) - l1p)

    q4 = q.reshape(M, BLK, SUBLANES, W)
    k4 = k.reshape(M, BLK, SUBLANES, W)
    v4 = v.reshape(M, BLK, SUBLANES, W)
    lf4 = (lf * LOG2_E).reshape(M, BLK, SUBLANES, W)

    order = list(range(BLK - 1, -1, -1)) if reverse else list(range(BLK))
    bs = [None] * BLK
    run = None
    for j in order:
        run = lf4[:, j] if run is None else run + lf4[:, j]
        bs[j] = run
    b_last = run

    def put(slab_ref, j, val):
        for p in range(n_pairs):
            for m in range(M):
                slab_ref[p, m * MICRO + j * SUBLANES:m * MICRO + (j + 1) * SUBLANES, :] = (
                    val[m, :, p * LANES:(p + 1) * LANES])

    for j in range(BLK):
        put(qin_s, j, q4[:, j] * jnp.exp2(bs[j]))
        put(kout_s, j, k4[:, j] * jnp.exp2(b_last - bs[j]))
        put(v_s, j, v4[:, j])
    dec = jnp.exp2(b_last).reshape(nblk, W)
    for p in range(n_pairs):
        dec_s[p] = dec[:, p * LANES:(p + 1) * LANES]

    ebd = ebd_ref[...]
    for t in range(BLK):
        srcs = [s for s in range(BLK) if (s >= t if reverse else s <= t)]
        prods = [(q4[:, t] * k4[:, s] * jnp.exp2(bs[t] - bs[s])).reshape(nblk, W) for s in srcs]
        scores = jnp.dot(jnp.concatenate(prods, axis=0).astype(BF16), ebd, preferred_element_type=F32)
        o_t = None
        for idx, s in enumerate(srcs):
            term = scores[idx * nblk:(idx + 1) * nblk, :] * v4[:, s].reshape(nblk, W)
            o_t = term if o_t is None else o_t + term
        put(o_s, t, o_t.reshape(M, SUBLANES, W))

    pmask = pmask_ref[...]
    block_rows = lambda blk: pl.ds((blk // SUBLANES) * MICRO + blk % SUBLANES, BLK, stride=SUBLANES)
    for blk in range(nblk):
        for p in range(n_pairs):
            kb = kout_s[p, block_rows(blk), :].astype(BF16)
            vb = v_s[p, block_rows(blk), :].astype(BF16)
            upd = lax.dot_general(vb, kb, (((0,), (0,)), ((), ())), preferred_element_type=F32)
            upd_s[p * nblk + blk] = upd * pmask
    for p in range(n_pairs):
        st = st_s[p]
        for blk in (range(nblk - 1, -1, -1) if reverse else range(nblk)):
            ent_s[p * nblk + blk] = st.astype(BF16)
            st = st * dec_s[p, blk:blk + 1, :] + upd_s[p * nblk + blk]
        st_s[p] = st
    for blk in range(nblk):
        for p in range(n_pairs):
            qb = qin_s[p, block_rows(blk), :].astype(BF16)
            o_inter = lax.dot_general(qb, ent_s[p * nblk + blk], (((1,), (1,)), ((), ())),
                                      preferred_element_type=F32)
            o_s[p, block_rows(blk), :] = o_s[p, block_rows(blk), :] + o_inter

    o = jnp.concatenate([o_s[p] for p in range(n_pairs)], axis=1)
    if reverse:
        out_ref[0] = o
    else:
        tot = o + ob_ref[0]
        msq = jnp.dot((tot * tot).astype(BF16), ebd, preferred_element_type=F32) * (1.0 / B_HEAD_DIM)
        xg = xg_ref[0]
        y = tot * lax.rsqrt(msq + EPS) * (xg * _sigmoid(xg))
        out_ref[0] = _rms(y, W) * gain_ref[...]


def _hg_call(pb, ob, lbp, ebd, pmask, gain, *, reverse, seq_len):
    B, Tp, _ = pb.shape
    n_tiles = Tp // SEQ_TILE
    t_of = (lambda i: n_tiles - 1 - i) if reverse else (lambda i: i)
    tile3 = lambda col: pl.BlockSpec((1, SEQ_TILE, B_WIDTH), lambda b, i: (b, t_of(i), col))
    full = lambda a: pl.BlockSpec(a.shape, lambda b, i: (0,) * a.ndim)
    n_pairs = B_WIDTH // LANES
    nblk = SEQ_TILE_MICRO * SUBLANES
    slab = pltpu.VMEM((n_pairs, SEQ_TILE, LANES), F32)
    scratch = [slab, slab, slab, slab, pltpu.VMEM((n_pairs, nblk, LANES), F32),
               pltpu.VMEM((n_pairs, LANES, LANES), F32),
               pltpu.VMEM((n_pairs * nblk, LANES, LANES), F32),
               pltpu.VMEM((n_pairs * nblk, LANES, LANES), BF16)]
    if reverse:
        args = (pb, pb, pb, lbp, ebd, pmask)
        specs = [tile3(0), tile3(1), tile3(3), full(lbp), full(ebd), full(pmask)]
    else:
        args = (pb, pb, pb, pb, ob, lbp, ebd, pmask, gain)
        specs = [tile3(0), tile3(1), tile3(2), tile3(4), tile3(0), full(lbp), full(ebd), full(pmask), full(gain)]
    return pl.pallas_call(
        functools.partial(_hg_kernel, reverse=reverse, seq_len=seq_len, n_tiles=n_tiles),
        grid=(B, n_tiles),
        in_specs=specs,
        out_specs=tile3(0),
        out_shape=jax.ShapeDtypeStruct((B, Tp, B_WIDTH), F32),
        scratch_shapes=scratch,
        compiler_params=_params(2, VMEM_LIMIT),
        name="hgrn2_bwd" if reverse else "hgrn2_fwd",
    )(*args)


def _granule_transpose(pieces, n_rows):
    gran = lax.broadcasted_iota(jnp.int32, (n_rows, LANES), 1) // C_GROUP
    per = LANES // C_GROUP
    out = []
    for g in range(C_GROUPS):
        halves = []
        for half in range(BLK // per):
            acc = None
            for jj in range(per):
                piece = pieces[half * per + jj][:, (g // per) * LANES:(g // per + 1) * LANES]
                shift = ((jj - g % per) * C_GROUP) % LANES
                if shift:
                    piece = pltpu.roll(piece, shift, axis=1)
                acc = piece if acc is None else jnp.where(gran == jj, piece, acc)
            halves.append(acc)
        out.append(jnp.concatenate(halves, axis=1))
    return out


def _s5_kernel(u_ref, tz_ref, r_ref, o_ref, arow_ref, y_ref, z_s, hs_s, yi_s, *, n_chunks, valid_chunks):
    n_micro = n_chunks // SUBLANES
    G = C_GROUPS
    chunk = lax.broadcasted_iota(jnp.int32, (n_chunks, 1), 0)
    x4 = u_ref[0].reshape(n_micro, BLK, SUBLANES, C_WIDTH)
    src = [jnp.where(chunk < valid_chunks, x4[:, j].reshape(n_chunks, C_WIDTH), 0.0) for j in range(BLK)]
    ug = _granule_transpose(src, n_chunks)
    for g in range(G):
        ub = ug[g].astype(BF16)
        yi_s[g] = jnp.dot(ub, tz_ref[g], preferred_element_type=F32)
        z = jnp.dot(ub, r_ref[g], preferred_element_type=F32)
        z_s[0, g * n_chunks:(g + 1) * n_chunks, :] = z[:, 0:LANES]
        z_s[1, g * n_chunks:(g + 1) * n_chunks, :] = z[:, LANES:2 * LANES]
    a_re_f, a_im_f, a_re_b, a_im_b = arow_ref[0], arow_ref[1], arow_ref[2], arow_ref[3]
    half = C_STATE

    def step(kf, carry):
        hf, hb = carry
        kb = n_chunks - 1 - kf
        rf = pl.ds(kf, G, stride=n_chunks)
        rb = pl.ds(kb, G, stride=n_chunks)
        hs_s[0, rf, :] = hf
        hs_s[1, rb, :] = hb
        hf = a_re_f * hf + a_im_f * pltpu.roll(hf, half, axis=1) + z_s[0, rf, :]
        hb = a_re_b * hb + a_im_b * pltpu.roll(hb, half, axis=1) + z_s[1, rb, :]
        return hf, hb

    zero = jnp.zeros((G, LANES), F32)
    lax.fori_loop(0, n_chunks, step, (zero, zero))
    yg = []
    for g in range(G):
        hs = jnp.concatenate([hs_s[0, g * n_chunks:(g + 1) * n_chunks, :],
                              hs_s[1, g * n_chunks:(g + 1) * n_chunks, :]], axis=1)
        yg.append(yi_s[g] + jnp.dot(hs.astype(BF16), o_ref[g], preferred_element_type=F32))
    yt = _granule_transpose(yg, n_chunks)
    y4 = jnp.stack([yt[j].reshape(n_micro, SUBLANES, C_WIDTH) for j in range(BLK)], axis=1)
    y_ref[0] = y4.reshape(n_chunks * BLK, C_WIDTH)


def _s5_call(u, tz, rmat, omat, arow, *, seq_len):
    B, Tp, _ = u.shape
    n_chunks = Tp // BLK
    seq = pl.BlockSpec((1, Tp, C_WIDTH), lambda b: (b, 0, 0))
    once = lambda a: pl.BlockSpec(a.shape, lambda b: (0,) * a.ndim, pipeline_mode=pl.Buffered(1))
    slab = pltpu.VMEM((2, C_GROUPS * n_chunks, LANES), F32)
    return pl.pallas_call(
        functools.partial(_s5_kernel, n_chunks=n_chunks, valid_chunks=seq_len // BLK),
        grid=(B,),
        in_specs=[seq, once(tz), once(rmat), once(omat), once(arow)],
        out_specs=seq,
        out_shape=jax.ShapeDtypeStruct(u.shape, F32),
        scratch_shapes=[slab, slab, pltpu.VMEM((C_GROUPS, n_chunks, BLK * C_GROUP), F32)],
        compiler_params=_params(1, VMEM_LIMIT),
        name="s5_chunked",
    )(u, tz, rmat, omat, arow)


def _cmul(x, y):
    return x[0] * y[0] - x[1] * y[1], x[0] * y[1] + x[1] * y[0]


def _s5_weights(a_re, a_im, log_dt, b_re, b_im, c_re, c_im):
    hp = lax.Precision.HIGHEST
    lam = (a_re.astype(F32), a_im.astype(F32))
    dt = jnp.exp(log_dt.astype(F32))[..., None]
    ld = (lam[0] * dt, lam[1] * dt)
    steps = jnp.arange(BLK + 1, dtype=F32)[:, None, None, None]
    mag = jnp.exp(steps * ld[0][None])
    apow = (mag * jnp.cos(steps * ld[1][None]), mag * jnp.sin(steps * ld[1][None]))
    num = (apow[0][1] - 1.0, apow[1][1])
    den = lam[0] * lam[0] + lam[1] * lam[1]
    coef = ((num[0] * lam[0] + num[1] * lam[1]) / den, (num[1] * lam[0] - num[0] * lam[1]) / den)
    bb = _cmul((coef[0][..., None], coef[1][..., None]), (b_re.astype(F32), b_im.astype(F32)))
    cc = (c_re.astype(F32), c_im.astype(F32))
    ab = _cmul((apow[0][:BLK, ..., None], apow[1][:BLK, ..., None]), (bb[0][None], bb[1][None]))
    kern = (jnp.einsum('dgop,ldgpi->dlgoi', cc[0], ab[0], precision=hp)
            - jnp.einsum('dgop,ldgpi->dlgoi', cc[1], ab[1], precision=hp))
    s_idx = jnp.arange(BLK)[:, None]
    t_idx = jnp.arange(BLK)[None, :]
    lag_f = t_idx - s_idx
    kf = jnp.where((lag_f >= 0)[:, :, None, None, None], kern[0][jnp.clip(lag_f, 0, BLK - 1)], 0.0)
    kb = jnp.where((lag_f <= 0)[:, :, None, None, None], kern[1][jnp.clip(-lag_f, 0, BLK - 1)], 0.0)
    tz = (kf + kb).transpose(2, 0, 4, 1, 3).reshape(C_GROUPS, BLK * C_GROUP, BLK * C_GROUP)
    pick = lambda idx, d: (apow[0][idx, d][:, :, None, :], apow[1][idx, d][:, :, None, :])
    bt = lambda d: (bb[0][d].transpose(0, 2, 1)[None], bb[1][d].transpose(0, 2, 1)[None])
    rf = _cmul(pick(BLK - 1 - jnp.arange(BLK), 0), bt(0))
    rb = _cmul(pick(jnp.arange(BLK), 1), bt(1))
    to_rows = lambda z: z.transpose(1, 0, 2, 3).reshape(C_GROUPS, BLK * C_GROUP, C_STATE)
    rmat = jnp.concatenate([to_rows(rf[0]), to_rows(rf[1]), to_rows(rb[0]), to_rows(rb[1])], axis=-1)
    wf = _cmul((cc[0][0][None], cc[1][0][None]), pick(1 + jnp.arange(BLK), 0))
    wb = _cmul((cc[0][1][None], cc[1][1][None]), pick(BLK - jnp.arange(BLK), 1))
    to_cols = lambda z: z.transpose(1, 3, 0, 2).reshape(C_GROUPS, C_STATE, BLK * C_GROUP)
    omat = jnp.concatenate([to_cols(wf[0]), to_cols(-wf[1]), to_cols(wb[0]), to_cols(-wb[1])], axis=1)
    re16, im16 = apow[0][BLK], apow[1][BLK]
    arow = jnp.stack([jnp.concatenate([re16[0], re16[0]], -1), jnp.concatenate([-im16[0], im16[0]], -1),
                      jnp.concatenate([re16[1], re16[1]], -1), jnp.concatenate([-im16[1], im16[1]], -1)])
    return tz.astype(BF16), rmat.astype(BF16), omat.astype(BF16), arow


def _block_diag(w):
    H, n, _ = w.shape
    eye = jnp.eye(H, dtype=w.dtype)
    return (eye[:, None, :, None] * w[:, :, None, :]).reshape(H * n, H * n)


def _to_blocked(x, padded_len):
    B, T, D = x.shape
    x = jnp.pad(x, ((0, 0), (0, padded_len - T), (0, 0)))
    x = x.reshape(B, padded_len // MICRO, SUBLANES, BLK, D).transpose(0, 1, 3, 2, 4)
    return x.reshape(B, padded_len, D)


def _from_blocked(y):
    B, Tp, D = y.shape
    y = y.reshape(B, Tp // MICRO, BLK, SUBLANES, D).transpose(0, 1, 3, 2, 4)
    return y.reshape(B, Tp, D)


def _trunk(x, P):
    B, L, _ = x.shape
    T = N_META + L
    Tp = -(-T // SEQ_TILE) * SEQ_TILE
    rows = B * Tp
    assert rows % ROW_TILE == 0 and T % BLK == 0
    meta = jnp.broadcast_to(P["meta_tokens"].astype(x.dtype)[None], (B, N_META, D_MODEL))
    h = _to_blocked(jnp.concatenate([meta, x], axis=1), Tp).reshape(rows, D_MODEL)
    for l in range(DEPTH):
        pa, pb, pc = _in_proj(h, P["norm_mix"][l], P["w_in"][l])
        ys = _s5_call(pc.reshape(B, Tp, C_WIDTH), *P["s5"][l], seq_len=T).reshape(rows, C_WIDTH)
        pa3 = pa.reshape(B, Tp, COL_A)
        rg = P["rg"][l]
        hb = _rg_call(pa3, None, rg["cw"], rg["cb"], rg["wg"][1], rg["bg"][1], rg["lam"][1], None,
                      reverse=True, seq_len=T)
        ya = _rg_call(pa3, hb, rg["cw"], rg["cb"], rg["wg"][0], rg["bg"][0], rg["lam"][0], P["gain_a"][l],
                      reverse=False, seq_len=T)
        pb3 = pb.reshape(B, Tp, COL_B)
        ob = _hg_call(pb3, None, P["lbp"][l][1], P["ebd"], P["pmask"], None, reverse=True, seq_len=T)
        yb = _hg_call(pb3, ob, P["lbp"][l][0], P["ebd"], P["pmask"], P["gain_b"][l], reverse=False, seq_len=T)
        h = _out_mlp(ya.reshape(rows, A_WIDTH), yb.reshape(rows, B_WIDTH), ys, pc, h,
                     P["s5_d"][l], P["glu_w"][l], P["glu_b"][l], P["gain_c"][l],
                     P["w_out"][l], P["norm_mlp"][l], P["w_up"][l], P["w_down"][l])
    y = _final_norm(h, P["norm_final"])
    return _from_blocked(y.reshape(B, Tp, D_MODEL))[:, N_META:T]


def _prepare(meta_tokens, norm_mix, w_in, conv_w, conv_b, rg_wr, rg_br, rg_wi, rg_bi, rg_lambda,
             hgrn_lb_logits, s5_a_re, s5_a_im, s5_log_dt, s5_b_re, s5_b_im, s5_c_re, s5_c_im, s5_d,
             s5_glu_w, s5_glu_b, mix_gain, w_out, norm_mlp, w_up, w_down, norm_final):
    row = lambda a: a.astype(F32)[:, None, :]
    lb_c = jnp.cumsum(jax.nn.softmax(hgrn_lb_logits.astype(F32), axis=1), axis=1)
    lb = (lb_c - lb_c[:, :1]).transpose(1, 0, 2)
    lbp = jnp.stack([jnp.log(lb), jnp.log1p(-lb), 1.0 - lb], axis=2)
    rg = []
    for l in range(DEPTH):
        wg = jnp.stack([jnp.concatenate([_block_diag(rg_wr[l, d]), _block_diag(rg_wi[l, d])], axis=1)
                        for d in range(2)]).astype(BF16)
        bg = jnp.concatenate([rg_br[l], rg_bi[l]], axis=-1).astype(F32)[:, None, :]
        rg.append(dict(cw=conv_w[l].astype(F32), cb=conv_b[l].astype(F32)[None], wg=wg, bg=bg,
                       lam=rg_lambda[l].astype(F32)[:, None, :]))
    head = jnp.arange(B_WIDTH) // B_HEAD_DIM
    ebd = (head[:, None] == head[None, :]).astype(BF16)
    half = jnp.arange(LANES) // B_HEAD_DIM
    pmask = (half[:, None] == half[None, :]).astype(F32)
    gain = mix_gain.astype(F32)
    return dict(
        meta_tokens=meta_tokens, norm_mix=row(norm_mix), w_in=w_in.astype(BF16), rg=rg, lbp=lbp, ebd=ebd,
        pmask=pmask,
        s5=[_s5_weights(s5_a_re[l], s5_a_im[l], s5_log_dt[l], s5_b_re[l], s5_b_im[l], s5_c_re[l], s5_c_im[l])
            for l in range(DEPTH)],
        s5_d=row(s5_d), glu_w=s5_glu_w.astype(BF16), glu_b=row(s5_glu_b),
        gain_a=row(gain[:, 0:A_WIDTH]), gain_b=row(gain[:, A_WIDTH:A_WIDTH + B_WIDTH]),
        gain_c=row(gain[:, A_WIDTH + B_WIDTH:]),
        w_out=w_out.astype(BF16), norm_mlp=row(norm_mlp), w_up=w_up.astype(BF16), w_down=w_down.astype(BF16),
        norm_final=norm_final.astype(F32)[None],
    )


def kernel(x_prompt, x_sample, meta_tokens, norm_mix, w_in, conv_w, conv_b, rg_wr, rg_br, rg_wi, rg_bi, rg_lambda, hgrn_lb_logits, s5_a_re, s5_a_im, s5_log_dt, s5_b_re, s5_b_im, s5_c_re, s5_c_im, s5_d, s5_glu_w, s5_glu_b, mix_gain, w_out, norm_mlp, w_up, w_down, norm_final):
    P = _prepare(meta_tokens, norm_mix, w_in, conv_w, conv_b, rg_wr, rg_br, rg_wi, rg_bi, rg_lambda,
                 hgrn_lb_logits, s5_a_re, s5_a_im, s5_log_dt, s5_b_re, s5_b_im, s5_c_re, s5_c_im, s5_d,
                 s5_glu_w, s5_glu_b, mix_gain, w_out, norm_mlp, w_up, w_down, norm_final)
    return (_trunk(x_prompt, P), _trunk(x_sample, P))
```

```python
import functools

import jax
import jax.numpy as jnp
from jax import lax
from jax.experimental import pallas as pl
from jax.experimental.pallas import tpu as pltpu

F32 = jnp.float32
BF16 = jnp.bfloat16

D_MODEL = 1024
DEPTH = 4
N_META = 16
A_WIDTH = 384
A_HEADS = 6
A_BLOCK = 64
CONV_WIDTH = 4
RG_C = 8.0
B_WIDTH = 384
B_HEADS = 6
B_HEAD_DIM = 64
C_WIDTH = 256
C_GROUP = 16
C_GROUPS = 16
C_STATE = 64
D_FF = 4 * D_MODEL
EPS = 1e-6
LOG2_E = 1.4426950408889634
COL_A = 2 * A_WIDTH
COL_B = 5 * B_WIDTH
D_IN = COL_A + COL_B + C_WIDTH

LANES = 128
SUBLANES = 8
BLK = 16
MICRO = SUBLANES * BLK
SEQ_TILE_MICRO = 3
SEQ_TILE = SEQ_TILE_MICRO * MICRO
ROW_TILE = 512
FF_CHUNK = 1024
VMEM_LIMIT = 56 * 1024 * 1024


def _params(n_axes, vmem=None):
    return pltpu.CompilerParams(dimension_semantics=("arbitrary",) * n_axes, vmem_limit_bytes=vmem)


def _rms(x, width):
    return x * lax.rsqrt(jnp.sum(x * x, axis=-1, keepdims=True) * (1.0 / width) + EPS)


def _sigmoid(x):
    return 1.0 / (1.0 + jnp.exp(-x))


def _token_index(tile, rows):
    r = lax.broadcasted_iota(jnp.int32, (rows, 1), 0)
    return tile * rows + (r // MICRO) * MICRO + (r % SUBLANES) * BLK + (r % MICRO) // SUBLANES


def _in_proj_kernel(h_ref, g_ref, w_ref, oa_ref, ob_ref, oc_ref):
    x = h_ref[...]
    y = (_rms(x, D_MODEL) * g_ref[...]).astype(BF16)
    oa_ref[...] = jnp.dot(y, w_ref[:, 0:COL_A], preferred_element_type=F32)
    ob_ref[...] = jnp.dot(y, w_ref[:, COL_A:COL_A + COL_B], preferred_element_type=F32)
    oc_ref[...] = jnp.dot(y, w_ref[:, COL_A + COL_B:D_IN], preferred_element_type=F32)


def _in_proj(h, gain, w):
    rows = h.shape[0]
    grid = (rows // ROW_TILE,)
    row = lambda c: pl.BlockSpec((ROW_TILE, c), lambda i: (i, 0))
    full = lambda a: pl.BlockSpec(a.shape, lambda i: (0,) * a.ndim)
    return pl.pallas_call(
        _in_proj_kernel,
        grid=grid,
        in_specs=[row(D_MODEL), full(gain), full(w)],
        out_specs=[row(COL_A), row(COL_B), row(C_WIDTH)],
        out_shape=[jax.ShapeDtypeStruct((rows, c), F32) for c in (COL_A, COL_B, C_WIDTH)],
        compiler_params=_params(1, VMEM_LIMIT),
        name="in_proj",
    )(h, gain, w)


def _out_mlp_kernel(ya_ref, yb_ref, ys_ref, u_ref, h_ref, dskip_ref, gluw_ref, glub_ref, gainc_ref,
                    wout_ref, nmlp_ref, wup_ref, wdown_ref, o_ref):
    y = ys_ref[...] + dskip_ref[...] * u_ref[...]
    z = jax.nn.gelu(y)
    gate = _sigmoid(jnp.dot(z.astype(BF16), gluw_ref[...], preferred_element_type=F32) + glub_ref[...])
    yc = _rms(z * gate, C_WIDTH) * gainc_ref[...]
    mix = jnp.dot(ya_ref[...].astype(BF16), wout_ref[0:A_WIDTH, :], preferred_element_type=F32)
    mix += jnp.dot(yb_ref[...].astype(BF16), wout_ref[A_WIDTH:A_WIDTH + B_WIDTH, :], preferred_element_type=F32)
    mix += jnp.dot(yc.astype(BF16), wout_ref[A_WIDTH + B_WIDTH:D_MODEL, :], preferred_element_type=F32)
    h1 = h_ref[...] + mix
    hn = (_rms(h1, D_MODEL) * nmlp_ref[...]).astype(BF16)
    acc = h1
    for c in range(D_FF // FF_CHUNK):
        up = jnp.dot(hn, wup_ref[:, c * FF_CHUNK:(c + 1) * FF_CHUNK], preferred_element_type=F32)
        act = jnp.square(jnp.maximum(up, 0.0)).astype(BF16)
        acc += jnp.dot(act, wdown_ref[c * FF_CHUNK:(c + 1) * FF_CHUNK, :], preferred_element_type=F32)
    o_ref[...] = acc


def _out_mlp(ya, yb, ys, u, h, dskip, gluw, glub, gainc, wout, nmlp, wup, wdown):
    rows = h.shape[0]
    grid = (rows // ROW_TILE,)
    row = lambda c: pl.BlockSpec((ROW_TILE, c), lambda i: (i, 0))
    full = lambda a: pl.BlockSpec(a.shape, lambda i: (0,) * a.ndim)
    once = lambda a: pl.BlockSpec(a.shape, lambda i: (0,) * a.ndim, pipeline_mode=pl.Buffered(1))
    return pl.pallas_call(
        _out_mlp_kernel,
        grid=grid,
        in_specs=[row(A_WIDTH), row(B_WIDTH), row(C_WIDTH), row(C_WIDTH), row(D_MODEL),
                  full(dskip), full(gluw), full(glub), full(gainc), once(wout), full(nmlp), once(wup), once(wdown)],
        out_specs=row(D_MODEL),
        out_shape=jax.ShapeDtypeStruct((rows, D_MODEL), F32),
        compiler_params=_params(1, VMEM_LIMIT),
        name="out_mlp",
    )(ya, yb, ys, u, h, dskip, gluw, glub, gainc, wout, nmlp, wup, wdown)


def _final_norm_kernel(h_ref, g_ref, o_ref):
    o_ref[...] = _rms(h_ref[...], D_MODEL) * g_ref[...]


def _final_norm(h, gain):
    rows = h.shape[0]
    row = pl.BlockSpec((ROW_TILE, D_MODEL), lambda i: (i, 0))
    return pl.pallas_call(
        _final_norm_kernel,
        grid=(rows // ROW_TILE,),
        in_specs=[row, pl.BlockSpec(gain.shape, lambda i: (0, 0))],
        out_specs=row,
        out_shape=jax.ShapeDtypeStruct((rows, D_MODEL), F32),
        compiler_params=_params(1),
        name="final_norm",
    )(h, gain)


def _rg_kernel(*refs, reverse, seq_len, n_tiles):
    if reverse:
        (xa_ref, prev_ref, next_ref, cw_ref, cb_ref, wg_ref, bg_ref, lam_ref,
         out_ref, blk_a, blk_h, blk_c, carry) = refs
    else:
        (xa_ref, ga_ref, hb_ref, prev_ref, next_ref, cw_ref, cb_ref, wg_ref, bg_ref, lam_ref, gain_ref,
         out_ref, blk_a, blk_h, blk_c, carry) = refs
    W = A_WIDTH
    M = SEQ_TILE_MICRO
    nblk = M * SUBLANES
    i = pl.program_id(1)
    tile = (n_tiles - 1 - i) if reverse else i

    @pl.when(i == 0)
    def _():
        carry[...] = jnp.zeros_like(carry)

    tok = _token_index(tile, SEQ_TILE)
    valid = tok < seq_len
    xa = jnp.where(valid, xa_ref[0], 0.0)
    x4 = xa.reshape(M, BLK, SUBLANES, W)
    xs = [x4[:, j] for j in range(BLK)]

    first_tok = tile * SEQ_TILE
    row_id = lax.broadcasted_iota(jnp.int32, (nblk, W), 0)

    def from_prev_block(src, halo_row, halo_tok):
        flat = src.reshape(nblk, W)
        halo = jnp.where(jnp.logical_and(halo_tok >= 0, halo_tok < seq_len), halo_row, 0.0)
        sh = pltpu.roll(flat, 1, axis=0)
        return jnp.where(row_id == 0, halo, sh).reshape(M, SUBLANES, W)

    def from_next_block(src, halo_row, halo_tok):
        flat = src.reshape(nblk, W)
        halo = jnp.where(halo_tok < seq_len, halo_row, 0.0)
        sh = pltpu.roll(flat, nblk - 1, axis=0)
        return jnp.where(row_id == nblk - 1, halo, sh).reshape(M, SUBLANES, W)

    p14 = from_prev_block(xs[BLK - 2], prev_ref[0, SUBLANES - 1:SUBLANES, :], first_tok - 2)
    p15 = from_prev_block(xs[BLK - 1], prev_ref[0, 2 * SUBLANES - 1:2 * SUBLANES, :], first_tok - 1)
    n0 = from_next_block(xs[0], next_ref[0, 0:1, :],
                         jnp.where(tile == n_tiles - 1, seq_len, first_tok + SEQ_TILE))
    ext = [p14, p15] + xs + [n0]
    cw = cw_ref[...]
    xc = [cb_ref[...] + cw[0:1] * ext[j] + cw[1:2] * ext[j + 1] + cw[2:3] * ext[j + 2] + cw[3:4] * ext[j + 3]
          for j in range(BLK)]
    xcf = jnp.stack(xc, axis=1).reshape(SEQ_TILE, W)

    g = jnp.dot(xcf.astype(BF16), wg_ref[...], preferred_element_type=F32) + bg_ref[...]
    r = _sigmoid(g[:, 0:W])
    ig = _sigmoid(g[:, W:2 * W])
    nl = -lam_ref[...]
    softplus = jnp.maximum(nl, 0.0) + jnp.log1p(jnp.exp(-jnp.abs(nl)))
    log_a = (-RG_C) * r * softplus
    a = jnp.exp(log_a)
    th = jnp.tanh(log_a)
    b = jnp.where(valid, jnp.sqrt(-2.0 * th / (1.0 - th)) * (ig * xcf), 0.0)
    a4 = a.reshape(M, BLK, SUBLANES, W)
    b4 = b.reshape(M, BLK, SUBLANES, W)

    order = list(range(BLK - 1, -1, -1)) if reverse else list(range(BLK))
    h_loc = [None] * BLK
    a_cum = [None] * BLK
    h = acc = None
    for j in order:
        if h is None:
            h, acc = b4[:, j], a4[:, j]
        else:
            h = a4[:, j] * h + b4[:, j]
            acc = a4[:, j] * acc
        h_loc[j], a_cum[j] = h, acc

    blk_a[...] = acc.reshape(nblk, W)
    blk_h[...] = h.reshape(nblk, W)
    c = carry[...]
    for k in (range(nblk - 1, -1, -1) if reverse else range(nblk)):
        blk_c[k:k + 1, :] = c
        c = blk_a[k:k + 1, :] * c + blk_h[k:k + 1, :]
    carry[...] = c
    cin = blk_c[...].reshape(M, SUBLANES, W)

    hs = jnp.stack([h_loc[j] + a_cum[j] * cin for j in range(BLK)], axis=1).reshape(SEQ_TILE, W)
    if reverse:
        out_ref[0] = hs
    else:
        y = (hs + hb_ref[0]) * jax.nn.gelu(ga_ref[0])
        out_ref[0] = _rms(y, W) * gain_ref[...]


def _rg_call(pa3, hb, cw, cb, wg, bg, lam, gain, *, reverse, seq_len):
    B, Tp, _ = pa3.shape
    n_tiles = Tp // SEQ_TILE
    M = SEQ_TILE_MICRO
    t_of = (lambda i: n_tiles - 1 - i) if reverse else (lambda i: i)
    tile3 = lambda col: pl.BlockSpec((1, SEQ_TILE, A_WIDTH), lambda b, i: (b, t_of(i), col))
    prev = pl.BlockSpec((1, 2 * SUBLANES, A_WIDTH),
                        lambda b, i: (b, jnp.maximum(t_of(i) * (SEQ_TILE // (2 * SUBLANES)) - 1, 0), 0))
    nxt = pl.BlockSpec((1, SUBLANES, A_WIDTH),
                       lambda b, i: (b, jnp.minimum((t_of(i) + 1) * (SEQ_TILE // SUBLANES), Tp // SUBLANES - 1), 0))
    full = lambda a: pl.BlockSpec(a.shape, lambda b, i: (0,) * a.ndim)
    nblk = M * SUBLANES
    scratch = [pltpu.VMEM((nblk, A_WIDTH), F32)] * 3 + [pltpu.VMEM((1, A_WIDTH), F32)]
    if reverse:
        args = (pa3, pa3, pa3, cw, cb, wg, bg, lam)
        specs = [tile3(0), prev, nxt, full(cw), full(cb), full(wg), full(bg), full(lam)]
    else:
        args = (pa3, pa3, hb, pa3, pa3, cw, cb, wg, bg, lam, gain)
        specs = [tile3(0), tile3(1), tile3(0), prev, nxt, full(cw), full(cb), full(wg), full(bg), full(lam),
                 full(gain)]
    return pl.pallas_call(
        functools.partial(_rg_kernel, reverse=reverse, seq_len=seq_len, n_tiles=n_tiles),
        grid=(B, n_tiles),
        in_specs=specs,
        out_specs=tile3(0),
        out_shape=jax.ShapeDtypeStruct((B, Tp, A_WIDTH), F32),
        scratch_shapes=scratch,
        compiler_params=_params(2, VMEM_LIMIT),
        name="rglru_bwd" if reverse else "rglru_fwd",
    )(*args)


def _hg_kernel(*refs, reverse, seq_len, n_tiles):
    if reverse:
        (xq_ref, xi_ref, xf_ref, lbp_ref, hsum_ref,
         out_ref, qin_s, kout_s, v_s, o_s, dec_s, st_s, upd_s, ent_s) = refs
    else:
        (xq_ref, xi_ref, xf_ref, xg_ref, ob_ref, lbp_ref, hsum_ref, gain_ref,
         out_ref, qin_s, kout_s, v_s, o_s, dec_s, st_s, upd_s, ent_s) = refs
    W = B_WIDTH
    M = SEQ_TILE_MICRO
    nblk = M * SUBLANES
    n_pairs = W // LANES
    i = pl.program_id(1)
    tile = (n_tiles - 1 - i) if reverse else i

    @pl.when(i == 0)
    def _():
        st_s[...] = jnp.zeros_like(st_s)

    valid = _token_index(tile, SEQ_TILE) < seq_len
    xq = xq_ref[0]
    q = xq * _sigmoid(xq)
    v = jnp.where(valid, xi_ref[0], 0.0)
    x = xf_ref[0]
    log_lb, log_1mlb, one_mlb = lbp_ref[0:1, :], lbp_ref[1:2, :], lbp_ref[2:3, :]
    l1p = jnp.log(1.0 + jnp.exp(-jnp.abs(x)))
    c = log_1mlb + (jnp.minimum(x, 0.0) - l1p)
    lf = jnp.maximum(log_lb, c) + jnp.log(1.0 + jnp.exp(-jnp.abs(log_lb - c)))
    k = one_mlb * jnp.exp(-jnp.maximum(x, 0.0) - l1p)

    q4 = q.reshape(M, BLK, SUBLANES, W)
    k4 = k.reshape(M, BLK, SUBLANES, W)
    v4 = v.reshape(M, BLK, SUBLANES, W)
    lf4 = (lf * LOG2_E).reshape(M, BLK, SUBLANES, W)

    order = list(range(BLK - 1, -1, -1)) if reverse else list(range(BLK))
    bs = [None] * BLK
    run = None
    for j in order:
        run = lf4[:, j] if run is None else run + lf4[:, j]
        bs[j] = run
    b_last = run

    def put(slab_ref, j, val):
        for p in range(n_pairs):
            for m in range(M):
                slab_ref[p, m * MICRO + j * SUBLANES:m * MICRO + (j + 1) * SUBLANES, :] = (
                    val[m, :, p * LANES:(p + 1) * LANES])

    for j in range(BLK):
        put(qin_s, j, q4[:, j] * jnp.exp2(bs[j]))
        put(kout_s, j, k4[:, j] * jnp.exp2(b_last - bs[j]))
        put(v_s, j, v4[:, j])
    dec = jnp.exp2(b_last).reshape(nblk, W)
    for p in range(n_pairs):
        dec_s[p] = dec[:, p * LANES:(p + 1) * LANES]

    hsum = hsum_ref[...]

    def head_sums(z):
        zb = z.astype(BF16)
        return jnp.concatenate([jnp.dot(zb[:, p * LANES:(p + 1) * LANES], hsum, preferred_element_type=F32)
                                for p in range(n_pairs)], axis=1)

    for t in range(BLK):
        srcs = [s for s in range(BLK) if (s >= t if reverse else s <= t)]
        prods = [(q4[:, t] * k4[:, s] * jnp.exp2(bs[t] - bs[s])).reshape(nblk, W) for s in srcs]
        scores = head_sums(jnp.concatenate(prods, axis=0))
        o_t = None
        for idx, s in enumerate(srcs):
            term = scores[idx * nblk:(idx + 1) * nblk, :] * v4[:, s].reshape(nblk, W)
            o_t = term if o_t is None else o_t + term
        put(o_s, t, o_t.reshape(M, SUBLANES, W))

    block_rows = lambda blk: pl.ds((blk // SUBLANES) * MICRO + blk % SUBLANES, BLK, stride=SUBLANES)
    first_head = lax.broadcasted_iota(jnp.int32, (BLK, LANES), 1) < B_HEAD_DIM

    def per_head(z):
        return jnp.concatenate([jnp.where(first_head, z, 0.0), jnp.where(first_head, 0.0, z)], axis=0).astype(BF16)

    for blk in range(nblk):
        for p in range(n_pairs):
            kb = per_head(kout_s[p, block_rows(blk), :])
            vb = per_head(v_s[p, block_rows(blk), :])
            upd_s[p * nblk + blk] = lax.dot_general(vb, kb, (((0,), (0,)), ((), ())), preferred_element_type=F32)
    for p in range(n_pairs):
        st = st_s[p]
        for blk in (range(nblk - 1, -1, -1) if reverse else range(nblk)):
            ent_s[p * nblk + blk] = st.astype(BF16)
            st = st * dec_s[p, blk:blk + 1, :] + upd_s[p * nblk + blk]
        st_s[p] = st
    for blk in range(nblk):
        for p in range(n_pairs):
            qb = qin_s[p, block_rows(blk), :].astype(BF16)
            o_inter = lax.dot_general(qb, ent_s[p * nblk + blk], (((1,), (1,)), ((), ())),
                                      preferred_element_type=F32)
            o_s[p, block_rows(blk), :] = o_s[p, block_rows(blk), :] + o_inter

    o = jnp.concatenate([o_s[p] for p in range(n_pairs)], axis=1)
    if reverse:
        out_ref[0] = o
    else:
        tot = o + ob_ref[0]
        msq = head_sums(tot * tot) * (1.0 / B_HEAD_DIM)
        xg = xg_ref[0]
        y = tot * lax.rsqrt(msq + EPS) * (xg * _sigmoid(xg))
        out_ref[0] = _rms(y, W) * gain_ref[...]


def _hg_call(pb, ob, lbp, hsum, gain, *, reverse, seq_len):
    B, Tp, _ = pb.shape
    n_tiles = Tp // SEQ_TILE
    t_of = (lambda i: n_tiles - 1 - i) if reverse else (lambda i: i)
    tile3 = lambda col: pl.BlockSpec((1, SEQ_TILE, B_WIDTH), lambda b, i: (b, t_of(i), col))
    full = lambda a: pl.BlockSpec(a.shape, lambda b, i: (0,) * a.ndim)
    n_pairs = B_WIDTH // LANES
    nblk = SEQ_TILE_MICRO * SUBLANES
    slab = pltpu.VMEM((n_pairs, SEQ_TILE, LANES), F32)
    scratch = [slab, slab, slab, slab, pltpu.VMEM((n_pairs, nblk, LANES), F32),
               pltpu.VMEM((n_pairs, LANES, LANES), F32),
               pltpu.VMEM((n_pairs * nblk, LANES, LANES), F32),
               pltpu.VMEM((n_pairs * nblk, LANES, LANES), BF16)]
    if reverse:
        args = (pb, pb, pb, lbp, hsum)
        specs = [tile3(0), tile3(1), tile3(3), full(lbp), full(hsum)]
    else:
        args = (pb, pb, pb, pb, ob, lbp, hsum, gain)
        specs = [tile3(0), tile3(1), tile3(2), tile3(4), tile3(0), full(lbp), full(hsum), full(gain)]
    return pl.pallas_call(
        functools.partial(_hg_kernel, reverse=reverse, seq_len=seq_len, n_tiles=n_tiles),
        grid=(B, n_tiles),
        in_specs=specs,
        out_specs=tile3(0),
        out_shape=jax.ShapeDtypeStruct((B, Tp, B_WIDTH), F32),
        scratch_shapes=scratch,
        compiler_params=_params(2, VMEM_LIMIT),
        name="hgrn2_bwd" if reverse else "hgrn2_fwd",
    )(*args)


def _granule_transpose(pieces, n_rows):
    per = LANES // C_GROUP
    gran = lax.broadcasted_iota(jnp.int32, (n_rows, LANES), 1) // C_GROUP
    out = [[None, None] for _ in range(C_GROUPS)]
    for hi_j in range(BLK // per):
        for hi_g in range(C_GROUPS // per):
            x = [pieces[hi_j * per + a][:, hi_g * LANES:(hi_g + 1) * LANES] for a in range(per)]
            d = 1
            while d < per:
                upper = (gran // d) % 2 == 1
                for a in range(per):
                    if a & d == 0:
                        lo, hi = x[a], x[a + d]
                        x[a] = jnp.where(upper, pltpu.roll(hi, d * C_GROUP, axis=1), lo)
                        x[a + d] = jnp.where(upper, hi, pltpu.roll(lo, LANES - d * C_GROUP, axis=1))
                d *= 2
            for a in range(per):
                out[hi_g * per + a][hi_j] = x[a]
    return [jnp.concatenate(halves, axis=1) for halves in out]


def _s5_kernel(u_ref, tz_ref, r_ref, o_ref, arow_ref, y_ref, z_s, yi_s, *, n_chunks, valid_chunks):
    n_micro = n_chunks // SUBLANES
    G = C_GROUPS
    chunk = lax.broadcasted_iota(jnp.int32, (n_chunks, 1), 0)
    x4 = u_ref[0].reshape(n_micro, BLK, SUBLANES, C_WIDTH)
    src = [jnp.where(chunk < valid_chunks, x4[:, j].reshape(n_chunks, C_WIDTH), 0.0) for j in range(BLK)]
    ug = _granule_transpose(src, n_chunks)
    n_slabs = 4
    for g in range(G):
        ub = ug[g].astype(BF16)
        yi_s[g] = jnp.dot(ub, tz_ref[g], preferred_element_type=F32)
        z = jnp.dot(ub, r_ref[g], preferred_element_type=F32)
        for c in range(n_slabs):
            z_s[c, g * n_chunks:(g + 1) * n_chunks, :] = z[:, c * LANES:(c + 1) * LANES]
    a_re_f, a_im_f, a_re_b, a_im_b = arow_ref[0], arow_ref[1], arow_ref[2], arow_ref[3]

    def step(kf, carry):
        fr, fi, br, bi = carry
        kb = n_chunks - 1 - kf
        rf = pl.ds(kf, G, stride=n_chunks)
        rb = pl.ds(kb, G, stride=n_chunks)
        z = (z_s[0, rf, :], z_s[1, rf, :], z_s[2, rb, :], z_s[3, rb, :])
        z_s[0, rf, :] = fr
        z_s[1, rf, :] = fi
        z_s[2, rb, :] = br
        z_s[3, rb, :] = bi
        return (a_re_f * fr - a_im_f * fi + z[0], a_im_f * fr + a_re_f * fi + z[1],
                a_re_b * br - a_im_b * bi + z[2], a_im_b * br + a_re_b * bi + z[3])

    zero = jnp.zeros((G, LANES), F32)
    lax.fori_loop(0, n_chunks, step, (zero,) * n_slabs, unroll=4)
    yg = []
    for g in range(G):
        hs = jnp.concatenate([z_s[c, g * n_chunks:(g + 1) * n_chunks, :] for c in range(n_slabs)], axis=1)
        yg.append(yi_s[g] + lax.dot_general(hs.astype(BF16), o_ref[g], (((1,), (1,)), ((), ())),
                                            preferred_element_type=F32))
    yt = _granule_transpose(yg, n_chunks)
    y4 = jnp.stack([yt[j].reshape(n_micro, SUBLANES, C_WIDTH) for j in range(BLK)], axis=1)
    y_ref[0] = y4.reshape(n_chunks * BLK, C_WIDTH)


def _s5_call(u, tz, rmat, omat, arow, *, seq_len):
    B, Tp, _ = u.shape
    n_chunks = Tp // BLK
    seq = pl.BlockSpec((1, Tp, C_WIDTH), lambda b: (b, 0, 0))
    once = lambda a: pl.BlockSpec(a.shape, lambda b: (0,) * a.ndim, pipeline_mode=pl.Buffered(1))
    slab = pltpu.VMEM((4, C_GROUPS * n_chunks, LANES), F32)
    return pl.pallas_call(
        functools.partial(_s5_kernel, n_chunks=n_chunks, valid_chunks=seq_len // BLK),
        grid=(B,),
        in_specs=[seq, once(tz), once(rmat), once(omat), once(arow)],
        out_specs=seq,
        out_shape=jax.ShapeDtypeStruct(u.shape, F32),
        scratch_shapes=[slab, pltpu.VMEM((C_GROUPS, n_chunks, BLK * C_GROUP), F32)],
        compiler_params=_params(1, VMEM_LIMIT),
        name="s5_chunked",
    )(u, tz, rmat, omat, arow)


def _cmul(x, y):
    return x[0] * y[0] - x[1] * y[1], x[0] * y[1] + x[1] * y[0]


def _s5_weights(a_re, a_im, log_dt, b_re, b_im, c_re, c_im):
    hp = lax.Precision.HIGHEST
    G, P, H = C_GROUPS, C_STATE, C_GROUP
    lam = (a_re.astype(F32), a_im.astype(F32))
    dt = jnp.exp(log_dt.astype(F32))[..., None]
    ld = (lam[0] * dt, lam[1] * dt)
    steps = jnp.arange(BLK + 1, dtype=F32)[None, None, :, None]
    mag = jnp.exp(steps * ld[0][:, :, None, :])
    ang = steps * ld[1][:, :, None, :]
    apow = (mag * jnp.cos(ang), mag * jnp.sin(ang))
    num = (apow[0][:, :, 1] - 1.0, apow[1][:, :, 1])
    den = lam[0] * lam[0] + lam[1] * lam[1]
    coef = ((num[0] * lam[0] + num[1] * lam[1]) / den, (num[1] * lam[0] - num[0] * lam[1]) / den)
    b_t = (jnp.swapaxes(b_re.astype(F32), -1, -2), jnp.swapaxes(b_im.astype(F32), -1, -2))
    bb = _cmul((coef[0][:, :, None, :], coef[1][:, :, None, :]), b_t)
    cc = (c_re.astype(F32), c_im.astype(F32))
    w = _cmul((apow[0][:, :, :, None, :], apow[1][:, :, :, None, :]),
              (cc[0][:, :, None], cc[1][:, :, None]))
    lags = lambda z: jnp.stack([z[0, :, :BLK], z[1, :, BLK - 1::-1]]).reshape(2, G, BLK * H, P)
    kern = (jnp.einsum('dgip,dgkp->dgik', bb[0], lags(w[0]), precision=hp)
            - jnp.einsum('dgip,dgkp->dgik', bb[1], lags(w[1]), precision=hp))
    side = (BLK - 1) * H
    strip = (jnp.pad(kern[1], ((0, 0), (0, 0), (0, side)))
             + jnp.pad(kern[0], ((0, 0), (0, 0), (side, 0))))
    tz = jnp.stack([strip[:, :, (BLK - 1 - s) * H:(BLK - 1 - s) * H + BLK * H] for s in range(BLK)], axis=1)
    tz = tz.reshape(G, BLK * H, BLK * H)
    slabs = lambda parts: jnp.concatenate(
        [jnp.pad(z, ((0, 0),) * (z.ndim - 1) + ((0, LANES - P),)) for z in parts], axis=-1)
    a_f = (apow[0][0, :, BLK - 1::-1], apow[1][0, :, BLK - 1::-1])
    a_b = (apow[0][1, :, :BLK], apow[1][1, :, :BLK])
    rf = _cmul((a_f[0][:, :, None], a_f[1][:, :, None]), (bb[0][0][:, None], bb[1][0][:, None]))
    rb = _cmul((a_b[0][:, :, None], a_b[1][:, :, None]), (bb[0][1][:, None], bb[1][1][:, None]))
    rmat = slabs([rf[0], rf[1], rb[0], rb[1]]).reshape(G, BLK * H, 4 * LANES)
    wf = (w[0][0, :, 1:BLK + 1], w[1][0, :, 1:BLK + 1])
    wb = (w[0][1, :, BLK:0:-1], w[1][1, :, BLK:0:-1])
    omat = slabs([wf[0], -wf[1], wb[0], -wb[1]]).reshape(G, BLK * H, 4 * LANES)
    arow = slabs([jnp.stack([apow[0][0, :, BLK], apow[1][0, :, BLK], apow[0][1, :, BLK], apow[1][1, :, BLK]])])
    return tz.astype(BF16), rmat.astype(BF16), omat.astype(BF16), arow


def _block_diag(w):
    H, n = w.shape[-3], w.shape[-1]
    eye = jnp.eye(H, dtype=w.dtype)
    return (eye[:, None, :, None] * w[..., :, :, None, :]).reshape(w.shape[:-3] + (H * n, H * n))


def _to_blocked(x, padded_len):
    B, T, D = x.shape
    x = jnp.pad(x, ((0, 0), (0, padded_len - T), (0, 0)))
    x = x.reshape(B, padded_len // MICRO, SUBLANES, BLK, D).transpose(0, 1, 3, 2, 4)
    return x.reshape(B, padded_len, D)


def _from_blocked(y):
    B, Tp, D = y.shape
    y = y.reshape(B, Tp // MICRO, BLK, SUBLANES, D).transpose(0, 1, 3, 2, 4)
    return y.reshape(B, Tp, D)


def _trunk(x, P):
    B, L, _ = x.shape
    T = N_META + L
    Tp = -(-T // SEQ_TILE) * SEQ_TILE
    rows = B * Tp
    assert rows % ROW_TILE == 0 and T % BLK == 0
    meta = jnp.broadcast_to(P["meta_tokens"].astype(x.dtype)[None], (B, N_META, D_MODEL))
    h = _to_blocked(jnp.concatenate([meta, x], axis=1), Tp).reshape(rows, D_MODEL)
    for l in range(DEPTH):
        pa, pb, pc = _in_proj(h, P["norm_mix"][l], P["w_in"][l])
        ys = _s5_call(pc.reshape(B, Tp, C_WIDTH), *P["s5"][l], seq_len=T).reshape(rows, C_WIDTH)
        pa3 = pa.reshape(B, Tp, COL_A)
        rg = P["rg"][l]
        hb = _rg_call(pa3, None, rg["cw"], rg["cb"], rg["wg"][1], rg["bg"][1], rg["lam"][1], None,
                      reverse=True, seq_len=T)
        ya = _rg_call(pa3, hb, rg["cw"], rg["cb"], rg["wg"][0], rg["bg"][0], rg["lam"][0], P["gain_a"][l],
                      reverse=False, seq_len=T)
        pb3 = pb.reshape(B, Tp, COL_B)
        ob = _hg_call(pb3, None, P["lbp"][l][1], P["hsum"], None, reverse=True, seq_len=T)
        yb = _hg_call(pb3, ob, P["lbp"][l][0], P["hsum"], P["gain_b"][l], reverse=False, seq_len=T)
        h = _out_mlp(ya.reshape(rows, A_WIDTH), yb.reshape(rows, B_WIDTH), ys, pc, h,
                     P["s5_d"][l], P["glu_w"][l], P["glu_b"][l], P["gain_c"][l],
                     P["w_out"][l], P["norm_mlp"][l], P["w_up"][l], P["w_down"][l])
    y = _final_norm(h, P["norm_final"])
    return _from_blocked(y.reshape(B, Tp, D_MODEL))[:, N_META:T]


def _prepare(meta_tokens, norm_mix, w_in, conv_w, conv_b, rg_wr, rg_br, rg_wi, rg_bi, rg_lambda,
             hgrn_lb_logits, s5_a_re, s5_a_im, s5_log_dt, s5_b_re, s5_b_im, s5_c_re, s5_c_im, s5_d,
             s5_glu_w, s5_glu_b, mix_gain, w_out, norm_mlp, w_up, w_down, norm_final):
    row = lambda a: a.astype(F32)[:, None, :]
    lb_c = jnp.cumsum(jax.nn.softmax(hgrn_lb_logits.astype(F32), axis=1), axis=1)
    lb = (lb_c - lb_c[:, :1]).transpose(1, 0, 2)
    lbp = jnp.stack([jnp.log(lb), jnp.log1p(-lb), 1.0 - lb], axis=2)
    wg_all = jnp.concatenate([_block_diag(rg_wr), _block_diag(rg_wi)], axis=-1).astype(BF16)
    bg_all = jnp.concatenate([rg_br, rg_bi], axis=-1).astype(F32)[:, :, None, :]
    rg = [dict(cw=conv_w[l].astype(F32), cb=conv_b[l].astype(F32)[None], wg=wg_all[l], bg=bg_all[l],
               lam=rg_lambda[l].astype(F32)[:, None, :]) for l in range(DEPTH)]
    half = jnp.arange(LANES) // B_HEAD_DIM
    hsum = (half[:, None] == half[None, :]).astype(BF16)
    gain = mix_gain.astype(F32)
    s5_all = jax.vmap(_s5_weights)(s5_a_re, s5_a_im, s5_log_dt, s5_b_re, s5_b_im, s5_c_re, s5_c_im)
    return dict(
        meta_tokens=meta_tokens, norm_mix=row(norm_mix), w_in=w_in.astype(BF16), rg=rg, lbp=lbp, hsum=hsum,
        s5=[tuple(z[l] for z in s5_all) for l in range(DEPTH)],
        s5_d=row(s5_d), glu_w=s5_glu_w.astype(BF16), glu_b=row(s5_glu_b),
        gain_a=row(gain[:, 0:A_WIDTH]), gain_b=row(gain[:, A_WIDTH:A_WIDTH + B_WIDTH]),
        gain_c=row(gain[:, A_WIDTH + B_WIDTH:]),
        w_out=w_out.astype(BF16), norm_mlp=row(norm_mlp), w_up=w_up.astype(BF16), w_down=w_down.astype(BF16),
        norm_final=norm_final.astype(F32)[None],
    )


def kernel(x_prompt, x_sample, meta_tokens, norm_mix, w_in, conv_w, conv_b, rg_wr, rg_br, rg_wi, rg_bi, rg_lambda, hgrn_lb_logits, s5_a_re, s5_a_im, s5_log_dt, s5_b_re, s5_b_im, s5_c_re, s5_c_im, s5_d, s5_glu_w, s5_glu_b, mix_gain, w_out, norm_mlp, w_up, w_down, norm_final):
    P = _prepare(meta_tokens, norm_mix, w_in, conv_w, conv_b, rg_wr, rg_br, rg_wi, rg_bi, rg_lambda,
                 hgrn_lb_logits, s5_a_re, s5_a_im, s5_log_dt, s5_b_re, s5_b_im, s5_c_re, s5_c_im, s5_d,
                 s5_glu_w, s5_glu_b, mix_gain, w_out, norm_mlp, w_up, w_down, norm_final)
    return (_trunk(x_prompt, P), _trunk(x_sample, P))
```

```python
import functools

import jax
import jax.numpy as jnp
from jax import lax
from jax.experimental import pallas as pl
from jax.experimental.pallas import tpu as pltpu

F32 = jnp.float32
BF16 = jnp.bfloat16

D_MODEL = 1024
DEPTH = 4
N_META = 16
A_WIDTH = 384
A_HEADS = 6
A_BLOCK = 64
CONV_WIDTH = 4
RG_C = 8.0
B_WIDTH = 384
B_HEADS = 6
B_HEAD_DIM = 64
C_WIDTH = 256
C_GROUP = 16
C_GROUPS = 16
C_STATE = 64
D_FF = 4 * D_MODEL
EPS = 1e-6
LOG2_E = 1.4426950408889634
COL_A = 2 * A_WIDTH
COL_B = 5 * B_WIDTH
D_IN = COL_A + COL_B + C_WIDTH

LANES = 128
SUBLANES = 8
BLK = 16
MICRO = SUBLANES * BLK
SEQ_TILE_MICRO = 3
SEQ_TILE = SEQ_TILE_MICRO * MICRO
ROW_TILE = 512
FF_CHUNK = 1024
VMEM_LIMIT = 56 * 1024 * 1024


def _params(n_axes, vmem=None):
    return pltpu.CompilerParams(dimension_semantics=("arbitrary",) * n_axes, vmem_limit_bytes=vmem)


def _rms(x, width):
    return x * lax.rsqrt(jnp.sum(x * x, axis=-1, keepdims=True) * (1.0 / width) + EPS)


def _sigmoid(x):
    return 1.0 / (1.0 + jnp.exp(-x))


def _token_index(tile, rows):
    r = lax.broadcasted_iota(jnp.int32, (rows, 1), 0)
    return tile * rows + (r // MICRO) * MICRO + (r % SUBLANES) * BLK + (r % MICRO) // SUBLANES


def _in_proj_kernel(h_ref, g_ref, w_ref, oa_ref, ob_ref, oc_ref):
    x = h_ref[...]
    y = (_rms(x, D_MODEL) * g_ref[...]).astype(BF16)
    oa_ref[...] = jnp.dot(y, w_ref[:, 0:COL_A], preferred_element_type=F32)
    ob_ref[...] = jnp.dot(y, w_ref[:, COL_A:COL_A + COL_B], preferred_element_type=F32)
    oc_ref[...] = jnp.dot(y, w_ref[:, COL_A + COL_B:D_IN], preferred_element_type=F32)


def _in_proj(h, gain, w):
    rows = h.shape[0]
    grid = (rows // ROW_TILE,)
    row = lambda c: pl.BlockSpec((ROW_TILE, c), lambda i: (i, 0))
    full = lambda a: pl.BlockSpec(a.shape, lambda i: (0,) * a.ndim)
    return pl.pallas_call(
        _in_proj_kernel,
        grid=grid,
        in_specs=[row(D_MODEL), full(gain), full(w)],
        out_specs=[row(COL_A), row(COL_B), row(C_WIDTH)],
        out_shape=[jax.ShapeDtypeStruct((rows, c), F32) for c in (COL_A, COL_B, C_WIDTH)],
        compiler_params=_params(1, VMEM_LIMIT),
        name="in_proj",
    )(h, gain, w)


def _out_mlp_kernel(ya_ref, yb_ref, ys_ref, u_ref, h_ref, dskip_ref, gluw_ref, glub_ref, gainc_ref,
                    wout_ref, nmlp_ref, wup_ref, wdown_ref, nfin_ref, o_ref, *, final):
    y = ys_ref[...] + dskip_ref[...] * u_ref[...]
    z = jax.nn.gelu(y)
    gate = _sigmoid(jnp.dot(z.astype(BF16), gluw_ref[...], preferred_element_type=F32) + glub_ref[...])
    yc = _rms(z * gate, C_WIDTH) * gainc_ref[...]
    mix = jnp.dot(ya_ref[...].astype(BF16), wout_ref[0:A_WIDTH, :], preferred_element_type=F32)
    mix += jnp.dot(yb_ref[...].astype(BF16), wout_ref[A_WIDTH:A_WIDTH + B_WIDTH, :], preferred_element_type=F32)
    mix += jnp.dot(yc.astype(BF16), wout_ref[A_WIDTH + B_WIDTH:D_MODEL, :], preferred_element_type=F32)
    h1 = h_ref[...] + mix
    hn = (_rms(h1, D_MODEL) * nmlp_ref[...]).astype(BF16)
    acc = h1
    for c in range(D_FF // FF_CHUNK):
        up = jnp.dot(hn, wup_ref[:, c * FF_CHUNK:(c + 1) * FF_CHUNK], preferred_element_type=F32)
        act = jnp.square(jnp.maximum(up, 0.0)).astype(BF16)
        acc += jnp.dot(act, wdown_ref[c * FF_CHUNK:(c + 1) * FF_CHUNK, :], preferred_element_type=F32)
    o_ref[...] = _rms(acc, D_MODEL) * nfin_ref[...] if final else acc


def _out_mlp(ya, yb, ys, u, h, dskip, gluw, glub, gainc, wout, nmlp, wup, wdown, nfin, *, final):
    rows = h.shape[0]
    grid = (rows // ROW_TILE,)
    row = lambda c: pl.BlockSpec((ROW_TILE, c), lambda i: (i, 0))
    full = lambda a: pl.BlockSpec(a.shape, lambda i: (0,) * a.ndim)
    once = lambda a: pl.BlockSpec(a.shape, lambda i: (0,) * a.ndim, pipeline_mode=pl.Buffered(1))
    return pl.pallas_call(
        functools.partial(_out_mlp_kernel, final=final),
        grid=grid,
        in_specs=[row(A_WIDTH), row(B_WIDTH), row(C_WIDTH), row(C_WIDTH), row(D_MODEL),
                  full(dskip), full(gluw), full(glub), full(gainc), once(wout), full(nmlp), once(wup), once(wdown),
                  full(nfin)],
        out_specs=row(D_MODEL),
        out_shape=jax.ShapeDtypeStruct((rows, D_MODEL), F32),
        compiler_params=_params(1, VMEM_LIMIT),
        name="out_mlp",
    )(ya, yb, ys, u, h, dskip, gluw, glub, gainc, wout, nmlp, wup, wdown, nfin)


def _rg_kernel(*refs, reverse, seq_len, n_tiles):
    if reverse:
        (xa_ref, prev_ref, next_ref, cw_ref, cb_ref, wg_ref, bg_ref, lam_ref,
         out_ref, blk_a, blk_h, blk_c, carry) = refs
    else:
        (xa_ref, ga_ref, hb_ref, prev_ref, next_ref, cw_ref, cb_ref, wg_ref, bg_ref, lam_ref, gain_ref,
         out_ref, blk_a, blk_h, blk_c, carry) = refs
    W = A_WIDTH
    M = SEQ_TILE_MICRO
    nblk = M * SUBLANES
    i = pl.program_id(1)
    tile = (n_tiles - 1 - i) if reverse else i

    @pl.when(i == 0)
    def _():
        carry[...] = jnp.zeros_like(carry)

    tok = _token_index(tile, SEQ_TILE)
    valid = tok < seq_len
    xa = jnp.where(valid, xa_ref[0], 0.0)
    x4 = xa.reshape(M, BLK, SUBLANES, W)
    xs = [x4[:, j] for j in range(BLK)]

    first_tok = tile * SEQ_TILE
    row_id = lax.broadcasted_iota(jnp.int32, (nblk, W), 0)

    def from_prev_block(src, halo_row, halo_tok):
        flat = src.reshape(nblk, W)
        halo = jnp.where(jnp.logical_and(halo_tok >= 0, halo_tok < seq_len), halo_row, 0.0)
        sh = pltpu.roll(flat, 1, axis=0)
        return jnp.where(row_id == 0, halo, sh).reshape(M, SUBLANES, W)

    def from_next_block(src, halo_row, halo_tok):
        flat = src.reshape(nblk, W)
        halo = jnp.where(halo_tok < seq_len, halo_row, 0.0)
        sh = pltpu.roll(flat, nblk - 1, axis=0)
        return jnp.where(row_id == nblk - 1, halo, sh).reshape(M, SUBLANES, W)

    p14 = from_prev_block(xs[BLK - 2], prev_ref[0, SUBLANES - 1:SUBLANES, :], first_tok - 2)
    p15 = from_prev_block(xs[BLK - 1], prev_ref[0, 2 * SUBLANES - 1:2 * SUBLANES, :], first_tok - 1)
    n0 = from_next_block(xs[0], next_ref[0, 0:1, :],
                         jnp.where(tile == n_tiles - 1, seq_len, first_tok + SEQ_TILE))
    ext = [p14, p15] + xs + [n0]
    cw = cw_ref[...]
    xc = [cb_ref[...] + cw[0:1] * ext[j] + cw[1:2] * ext[j + 1] + cw[2:3] * ext[j + 2] + cw[3:4] * ext[j + 3]
          for j in range(BLK)]
    xcf = jnp.stack(xc, axis=1).reshape(SEQ_TILE, W)

    g = jnp.dot(xcf.astype(BF16), wg_ref[...], preferred_element_type=F32) + bg_ref[...]
    r = _sigmoid(g[:, 0:W])
    ig = _sigmoid(g[:, W:2 * W])
    nl = -lam_ref[...]
    softplus = jnp.maximum(nl, 0.0) + jnp.log1p(jnp.exp(-jnp.abs(nl)))
    log_a = (-RG_C) * r * softplus
    a = jnp.exp(log_a)
    th = jnp.tanh(log_a)
    b = jnp.where(valid, jnp.sqrt(-2.0 * th / (1.0 - th)) * (ig * xcf), 0.0)
    a4 = a.reshape(M, BLK, SUBLANES, W)
    b4 = b.reshape(M, BLK, SUBLANES, W)

    order = list(range(BLK - 1, -1, -1)) if reverse else list(range(BLK))
    h_loc = [None] * BLK
    a_cum = [None] * BLK
    h = acc = None
    for j in order:
        if h is None:
            h, acc = b4[:, j], a4[:, j]
        else:
            h = a4[:, j] * h + b4[:, j]
            acc = a4[:, j] * acc
        h_loc[j], a_cum[j] = h, acc

    blk_a[...] = acc.reshape(nblk, W)
    blk_h[...] = h.reshape(nblk, W)
    c = carry[...]
    for k in (range(nblk - 1, -1, -1) if reverse else range(nblk)):
        blk_c[k:k + 1, :] = c
        c = blk_a[k:k + 1, :] * c + blk_h[k:k + 1, :]
    carry[...] = c
    cin = blk_c[...].reshape(M, SUBLANES, W)

    hs = jnp.stack([h_loc[j] + a_cum[j] * cin for j in range(BLK)], axis=1).reshape(SEQ_TILE, W)
    if reverse:
        out_ref[0] = hs
    else:
        y = (hs + hb_ref[0]) * jax.nn.gelu(ga_ref[0])
        out_ref[0] = _rms(y, W) * gain_ref[...]


def _rg_call(pa3, hb, cw, cb, wg, bg, lam, gain, *, reverse, seq_len):
    B, Tp, _ = pa3.shape
    n_tiles = Tp // SEQ_TILE
    M = SEQ_TILE_MICRO
    t_of = (lambda i: n_tiles - 1 - i) if reverse else (lambda i: i)
    tile3 = lambda col: pl.BlockSpec((1, SEQ_TILE, A_WIDTH), lambda b, i: (b, t_of(i), col))
    prev = pl.BlockSpec((1, 2 * SUBLANES, A_WIDTH),
                        lambda b, i: (b, jnp.maximum(t_of(i) * (SEQ_TILE // (2 * SUBLANES)) - 1, 0), 0))
    nxt = pl.BlockSpec((1, SUBLANES, A_WIDTH),
                       lambda b, i: (b, jnp.minimum((t_of(i) + 1) * (SEQ_TILE // SUBLANES), Tp // SUBLANES - 1), 0))
    full = lambda a: pl.BlockSpec(a.shape, lambda b, i: (0,) * a.ndim)
    nblk = M * SUBLANES
    scratch = [pltpu.VMEM((nblk, A_WIDTH), F32)] * 3 + [pltpu.VMEM((1, A_WIDTH), F32)]
    if reverse:
        args = (pa3, pa3, pa3, cw, cb, wg, bg, lam)
        specs = [tile3(0), prev, nxt, full(cw), full(cb), full(wg), full(bg), full(lam)]
    else:
        args = (pa3, pa3, hb, pa3, pa3, cw, cb, wg, bg, lam, gain)
        specs = [tile3(0), tile3(1), tile3(0), prev, nxt, full(cw), full(cb), full(wg), full(bg), full(lam),
                 full(gain)]
    return pl.pallas_call(
        functools.partial(_rg_kernel, reverse=reverse, seq_len=seq_len, n_tiles=n_tiles),
        grid=(B, n_tiles),
        in_specs=specs,
        out_specs=tile3(0),
        out_shape=jax.ShapeDtypeStruct((B, Tp, A_WIDTH), F32),
        scratch_shapes=scratch,
        compiler_params=_params(2, VMEM_LIMIT),
        name="rglru_bwd" if reverse else "rglru_fwd",
    )(*args)


def _hg_kernel(*refs, reverse, seq_len, n_tiles):
    if reverse:
        (xq_ref, xi_ref, xf_ref, lbp_ref, hsum_ref,
         out_ref, qin_s, kout_s, v_s, o_s, dec_s, st_s, upd_s, ent_s) = refs
    else:
        (xq_ref, xi_ref, xf_ref, xg_ref, ob_ref, lbp_ref, hsum_ref, gain_ref,
         out_ref, qin_s, kout_s, v_s, o_s, dec_s, st_s, upd_s, ent_s) = refs
    W = B_WIDTH
    M = SEQ_TILE_MICRO
    nblk = M * SUBLANES
    n_pairs = W // LANES
    i = pl.program_id(1)
    tile = (n_tiles - 1 - i) if reverse else i

    @pl.when(i == 0)
    def _():
        st_s[...] = jnp.zeros_like(st_s)

    valid = _token_index(tile, SEQ_TILE) < seq_len
    xq = xq_ref[0]
    q = xq * _sigmoid(xq)
    v = jnp.where(valid, xi_ref[0], 0.0)
    x = xf_ref[0]
    log_lb, log_1mlb, one_mlb = lbp_ref[0:1, :], lbp_ref[1:2, :], lbp_ref[2:3, :]
    l1p = jnp.log(1.0 + jnp.exp(-jnp.abs(x)))
    c = log_1mlb + (jnp.minimum(x, 0.0) - l1p)
    lf = jnp.maximum(log_lb, c) + jnp.log(1.0 + jnp.exp(-jnp.abs(log_lb - c)))
    k = one_mlb * jnp.exp(-jnp.maximum(x, 0.0) - l1p)

    q4 = q.reshape(M, BLK, SUBLANES, W)
    k4 = k.reshape(M, BLK, SUBLANES, W)
    v4 = v.reshape(M, BLK, SUBLANES, W)
    lf4 = (lf * LOG2_E).reshape(M, BLK, SUBLANES, W)

    order = list(range(BLK - 1, -1, -1)) if reverse else list(range(BLK))
    bs = [None] * BLK
    run = None
    for j in order:
        run = lf4[:, j] if run is None else run + lf4[:, j]
        bs[j] = run
    b_last = run

    def put(slab_ref, j, val):
        for p in range(n_pairs):
            for m in range(M):
                slab_ref[p, m * MICRO + j * SUBLANES:m * MICRO + (j + 1) * SUBLANES, :] = (
                    val[m, :, p * LANES:(p + 1) * LANES])

    for j in range(BLK):
        put(qin_s, j, q4[:, j] * jnp.exp2(bs[j]))
        put(kout_s, j, k4[:, j] * jnp.exp2(b_last - bs[j]))
        put(v_s, j, v4[:, j])
    dec = jnp.exp2(b_last).reshape(nblk, W)
    for p in range(n_pairs):
        dec_s[p] = dec[:, p * LANES:(p + 1) * LANES]

    hsum = hsum_ref[...]

    def head_sums(z):
        zb = z.astype(BF16)
        return jnp.concatenate([jnp.dot(zb[:, p * LANES:(p + 1) * LANES], hsum, preferred_element_type=F32)
                                for p in range(n_pairs)], axis=1)

    for t in range(BLK):
        srcs = [s for s in range(BLK) if (s >= t if reverse else s <= t)]
        prods = [(q4[:, t] * k4[:, s] * jnp.exp2(bs[t] - bs[s])).reshape(nblk, W) for s in srcs]
        scores = head_sums(jnp.concatenate(prods, axis=0))
        o_t = None
        for idx, s in enumerate(srcs):
            term = scores[idx * nblk:(idx + 1) * nblk, :] * v4[:, s].reshape(nblk, W)
            o_t = term if o_t is None else o_t + term
        put(o_s, t, o_t.reshape(M, SUBLANES, W))

    block_rows = lambda blk: pl.ds((blk // SUBLANES) * MICRO + blk % SUBLANES, BLK, stride=SUBLANES)
    first_head = lax.broadcasted_iota(jnp.int32, (BLK, LANES), 1) < B_HEAD_DIM

    def per_head(z):
        return jnp.concatenate([jnp.where(first_head, z, 0.0), jnp.where(first_head, 0.0, z)], axis=0).astype(BF16)

    for blk in range(nblk):
        for p in range(n_pairs):
            kb = per_head(kout_s[p, block_rows(blk), :])
            vb = per_head(v_s[p, block_rows(blk), :])
            upd_s[p * nblk + blk] = lax.dot_general(vb, kb, (((0,), (0,)), ((), ())), preferred_element_type=F32)
    for p in range(n_pairs):
        st = st_s[p]
        for blk in (range(nblk - 1, -1, -1) if reverse else range(nblk)):
            ent_s[p * nblk + blk] = st.astype(BF16)
            st = st * dec_s[p, blk:blk + 1, :] + upd_s[p * nblk + blk]
        st_s[p] = st
    for blk in range(nblk):
        for p in range(n_pairs):
            qb = qin_s[p, block_rows(blk), :].astype(BF16)
            o_inter = lax.dot_general(qb, ent_s[p * nblk + blk], (((1,), (1,)), ((), ())),
                                      preferred_element_type=F32)
            o_s[p, block_rows(blk), :] = o_s[p, block_rows(blk), :] + o_inter

    o = jnp.concatenate([o_s[p] for p in range(n_pairs)], axis=1)
    if reverse:
        out_ref[0] = o
    else:
        tot = o + ob_ref[0]
        msq = head_sums(tot * tot) * (1.0 / B_HEAD_DIM)
        xg = xg_ref[0]
        y = tot * lax.rsqrt(msq + EPS) * (xg * _sigmoid(xg))
        out_ref[0] = _rms(y, W) * gain_ref[...]


def _hg_call(pb, ob, lbp, hsum, gain, *, reverse, seq_len):
    B, Tp, _ = pb.shape
    n_tiles = Tp // SEQ_TILE
    t_of = (lambda i: n_tiles - 1 - i) if reverse else (lambda i: i)
    tile3 = lambda col: pl.BlockSpec((1, SEQ_TILE, B_WIDTH), lambda b, i: (b, t_of(i), col))
    full = lambda a: pl.BlockSpec(a.shape, lambda b, i: (0,) * a.ndim)
    n_pairs = B_WIDTH // LANES
    nblk = SEQ_TILE_MICRO * SUBLANES
    slab = pltpu.VMEM((n_pairs, SEQ_TILE, LANES), F32)
    scratch = [slab, slab, slab, slab, pltpu.VMEM((n_pairs, nblk, LANES), F32),
               pltpu.VMEM((n_pairs, LANES, LANES), F32),
               pltpu.VMEM((n_pairs * nblk, LANES, LANES), F32),
               pltpu.VMEM((n_pairs * nblk, LANES, LANES), BF16)]
    if reverse:
        args = (pb, pb, pb, lbp, hsum)
        specs = [tile3(0), tile3(1), tile3(3), full(lbp), full(hsum)]
    else:
        args = (pb, pb, pb, pb, ob, lbp, hsum, gain)
        specs = [tile3(0), tile3(1), tile3(2), tile3(4), tile3(0), full(lbp), full(hsum), full(gain)]
    return pl.pallas_call(
        functools.partial(_hg_kernel, reverse=reverse, seq_len=seq_len, n_tiles=n_tiles),
        grid=(B, n_tiles),
        in_specs=specs,
        out_specs=tile3(0),
        out_shape=jax.ShapeDtypeStruct((B, Tp, B_WIDTH), F32),
        scratch_shapes=scratch,
        compiler_params=_params(2, VMEM_LIMIT),
        name="hgrn2_bwd" if reverse else "hgrn2_fwd",
    )(*args)


def _granule_transpose(pieces, n_rows):
    per = LANES // C_GROUP
    gran = lax.broadcasted_iota(jnp.int32, (n_rows, LANES), 1) // C_GROUP
    out = [[None, None] for _ in range(C_GROUPS)]
    for hi_j in range(BLK // per):
        for hi_g in range(C_GROUPS // per):
            x = [pieces[hi_j * per + a][:, hi_g * LANES:(hi_g + 1) * LANES] for a in range(per)]
            d = 1
            while d < per:
                upper = (gran // d) % 2 == 1
                for a in range(per):
                    if a & d == 0:
                        lo, hi = x[a], x[a + d]
                        x[a] = jnp.where(upper, pltpu.roll(hi, d * C_GROUP, axis=1), lo)
                        x[a + d] = jnp.where(upper, hi, pltpu.roll(lo, LANES - d * C_GROUP, axis=1))
                d *= 2
            for a in range(per):
                out[hi_g * per + a][hi_j] = x[a]
    return [jnp.concatenate(halves, axis=1) for halves in out]


def _s5_kernel(u_ref, tz_ref, r_ref, o_ref, arow_ref, y_ref, z_s, yi_s, *, n_chunks, valid_chunks):
    n_micro = n_chunks // SUBLANES
    G = C_GROUPS
    chunk = lax.broadcasted_iota(jnp.int32, (n_chunks, 1), 0)
    x4 = u_ref[0].reshape(n_micro, BLK, SUBLANES, C_WIDTH)
    src = [jnp.where(chunk < valid_chunks, x4[:, j].reshape(n_chunks, C_WIDTH), 0.0) for j in range(BLK)]
    ug = _granule_transpose(src, n_chunks)
    n_slabs = 4
    for g in range(G):
        ub = ug[g].astype(BF16)
        yi_s[g] = jnp.dot(ub, tz_ref[g], preferred_element_type=F32)
        z = jnp.dot(ub, r_ref[g], preferred_element_type=F32)
        for c in range(n_slabs):
            z_s[c, g * n_chunks:(g + 1) * n_chunks, :] = z[:, c * LANES:(c + 1) * LANES]
    a_re_f, a_im_f, a_re_b, a_im_b = arow_ref[0], arow_ref[1], arow_ref[2], arow_ref[3]

    def step(kf, carry):
        fr, fi, br, bi = carry
        kb = n_chunks - 1 - kf
        rf = pl.ds(kf, G, stride=n_chunks)
        rb = pl.ds(kb, G, stride=n_chunks)
        z = (z_s[0, rf, :], z_s[1, rf, :], z_s[2, rb, :], z_s[3, rb, :])
        z_s[0, rf, :] = fr
        z_s[1, rf, :] = fi
        z_s[2, rb, :] = br
        z_s[3, rb, :] = bi
        return (a_re_f * fr - a_im_f * fi + z[0], a_im_f * fr + a_re_f * fi + z[1],
                a_re_b * br - a_im_b * bi + z[2], a_im_b * br + a_re_b * bi + z[3])

    zero = jnp.zeros((G, LANES), F32)
    lax.fori_loop(0, n_chunks, step, (zero,) * n_slabs, unroll=4)
    yg = []
    for g in range(G):
        hs = jnp.concatenate([z_s[c, g * n_chunks:(g + 1) * n_chunks, :] for c in range(n_slabs)], axis=1)
        yg.append(yi_s[g] + lax.dot_general(hs.astype(BF16), o_ref[g], (((1,), (1,)), ((), ())),
                                            preferred_element_type=F32))
    yt = _granule_transpose(yg, n_chunks)
    y4 = jnp.stack([yt[j].reshape(n_micro, SUBLANES, C_WIDTH) for j in range(BLK)], axis=1)
    y_ref[0] = y4.reshape(n_chunks * BLK, C_WIDTH)


def _s5_call(u, tz, rmat, omat, arow, *, seq_len):
    B, Tp, _ = u.shape
    n_chunks = Tp // BLK
    seq = pl.BlockSpec((1, Tp, C_WIDTH), lambda b: (b, 0, 0))
    once = lambda a: pl.BlockSpec(a.shape, lambda b: (0,) * a.ndim, pipeline_mode=pl.Buffered(1))
    slab = pltpu.VMEM((4, C_GROUPS * n_chunks, LANES), F32)
    return pl.pallas_call(
        functools.partial(_s5_kernel, n_chunks=n_chunks, valid_chunks=seq_len // BLK),
        grid=(B,),
        in_specs=[seq, once(tz), once(rmat), once(omat), once(arow)],
        out_specs=seq,
        out_shape=jax.ShapeDtypeStruct(u.shape, F32),
        scratch_shapes=[slab, pltpu.VMEM((C_GROUPS, n_chunks, BLK * C_GROUP), F32)],
        compiler_params=_params(1, VMEM_LIMIT),
        name="s5_chunked",
    )(u, tz, rmat, omat, arow)


def _cmul(x, y):
    return x[0] * y[0] - x[1] * y[1], x[0] * y[1] + x[1] * y[0]


def _s5_weights(a_re, a_im, log_dt, b_re, b_im, c_re, c_im):
    hp = lax.Precision.HIGHEST
    G, P, H = C_GROUPS, C_STATE, C_GROUP
    lam = (a_re.astype(F32), a_im.astype(F32))
    dt = jnp.exp(log_dt.astype(F32))[..., None]
    ld = (lam[0] * dt, lam[1] * dt)
    steps = jnp.arange(BLK + 1, dtype=F32)[None, None, :, None]
    mag = jnp.exp(steps * ld[0][:, :, None, :])
    ang = steps * ld[1][:, :, None, :]
    apow = (mag * jnp.cos(ang), mag * jnp.sin(ang))
    num = (apow[0][:, :, 1] - 1.0, apow[1][:, :, 1])
    den = lam[0] * lam[0] + lam[1] * lam[1]
    coef = ((num[0] * lam[0] + num[1] * lam[1]) / den, (num[1] * lam[0] - num[0] * lam[1]) / den)
    b_t = (jnp.swapaxes(b_re.astype(F32), -1, -2), jnp.swapaxes(b_im.astype(F32), -1, -2))
    bb = _cmul((coef[0][:, :, None, :], coef[1][:, :, None, :]), b_t)
    cc = (c_re.astype(F32), c_im.astype(F32))
    w = _cmul((apow[0][:, :, :, None, :], apow[1][:, :, :, None, :]),
              (cc[0][:, :, None], cc[1][:, :, None]))
    lags = lambda z: jnp.stack([z[0, :, :BLK], z[1, :, BLK - 1::-1]]).reshape(2, G, BLK * H, P)
    kern = (jnp.einsum('dgip,dgkp->dgik', bb[0], lags(w[0]), precision=hp)
            - jnp.einsum('dgip,dgkp->dgik', bb[1], lags(w[1]), precision=hp))
    side = (BLK - 1) * H
    strip = (jnp.pad(kern[1], ((0, 0), (0, 0), (0, side)))
             + jnp.pad(kern[0], ((0, 0), (0, 0), (side, 0))))
    tz = jnp.stack([strip[:, :, (BLK - 1 - s) * H:(BLK - 1 - s) * H + BLK * H] for s in range(BLK)], axis=1)
    tz = tz.reshape(G, BLK * H, BLK * H)
    slabs = lambda parts: jnp.concatenate(
        [jnp.pad(z, ((0, 0),) * (z.ndim - 1) + ((0, LANES - P),)) for z in parts], axis=-1)
    a_f = (apow[0][0, :, BLK - 1::-1], apow[1][0, :, BLK - 1::-1])
    a_b = (apow[0][1, :, :BLK], apow[1][1, :, :BLK])
    rf = _cmul((a_f[0][:, :, None], a_f[1][:, :, None]), (bb[0][0][:, None], bb[1][0][:, None]))
    rb = _cmul((a_b[0][:, :, None], a_b[1][:, :, None]), (bb[0][1][:, None], bb[1][1][:, None]))
    rmat = slabs([rf[0], rf[1], rb[0], rb[1]]).reshape(G, BLK * H, 4 * LANES)
    wf = (w[0][0, :, 1:BLK + 1], w[1][0, :, 1:BLK + 1])
    wb = (w[0][1, :, BLK:0:-1], w[1][1, :, BLK:0:-1])
    omat = slabs([wf[0], -wf[1], wb[0], -wb[1]]).reshape(G, BLK * H, 4 * LANES)
    arow = slabs([jnp.stack([apow[0][0, :, BLK], apow[1][0, :, BLK], apow[0][1, :, BLK], apow[1][1, :, BLK]])])
    return tz.astype(BF16), rmat.astype(BF16), omat.astype(BF16), arow


def _block_diag(w):
    H, n = w.shape[-3], w.shape[-1]
    eye = jnp.eye(H, dtype=w.dtype)
    return (eye[:, None, :, None] * w[..., :, :, None, :]).reshape(w.shape[:-3] + (H * n, H * n))


def _to_blocked(x, padded_len):
    B, T, D = x.shape
    x = jnp.pad(x, ((0, 0), (0, padded_len - T), (0, 0)))
    x = x.reshape(B, padded_len // MICRO, SUBLANES, BLK, D).transpose(0, 1, 3, 2, 4)
    return x.reshape(B, padded_len, D)


def _from_blocked(y):
    B, Tp, D = y.shape
    y = y.reshape(B, Tp // MICRO, BLK, SUBLANES, D).transpose(0, 1, 3, 2, 4)
    return y.reshape(B, Tp, D)


def _trunk(x, P):
    B, L, _ = x.shape
    T = N_META + L
    Tp = -(-T // SEQ_TILE) * SEQ_TILE
    rows = B * Tp
    assert rows % ROW_TILE == 0 and T % BLK == 0
    meta = jnp.broadcast_to(P["meta_tokens"].astype(x.dtype)[None], (B, N_META, D_MODEL))
    h = _to_blocked(jnp.concatenate([meta, x], axis=1), Tp).reshape(rows, D_MODEL)
    for l in range(DEPTH):
        pa, pb, pc = _in_proj(h, P["norm_mix"][l], P["w_in"][l])
        ys = _s5_call(pc.reshape(B, Tp, C_WIDTH), *P["s5"][l], seq_len=T).reshape(rows, C_WIDTH)
        pa3 = pa.reshape(B, Tp, COL_A)
        rg = P["rg"][l]
        hb = _rg_call(pa3, None, rg["cw"], rg["cb"], rg["wg"][1], rg["bg"][1], rg["lam"][1], None,
                      reverse=True, seq_len=T)
        ya = _rg_call(pa3, hb, rg["cw"], rg["cb"], rg["wg"][0], rg["bg"][0], rg["lam"][0], P["gain_a"][l],
                      reverse=False, seq_len=T)
        pb3 = pb.reshape(B, Tp, COL_B)
        ob = _hg_call(pb3, None, P["lbp"][l][1], P["hsum"], None, reverse=True, seq_len=T)
        yb = _hg_call(pb3, ob, P["lbp"][l][0], P["hsum"], P["gain_b"][l], reverse=False, seq_len=T)
        h = _out_mlp(ya.reshape(rows, A_WIDTH), yb.reshape(rows, B_WIDTH), ys, pc, h,
                     P["s5_d"][l], P["glu_w"][l], P["glu_b"][l], P["gain_c"][l],
                     P["w_out"][l], P["norm_mlp"][l], P["w_up"][l], P["w_down"][l], P["norm_final"],
                     final=(l == DEPTH - 1))
    y = h
    return _from_blocked(y.reshape(B, Tp, D_MODEL))[:, N_META:T]


def _prepare(meta_tokens, norm_mix, w_in, conv_w, conv_b, rg_wr, rg_br, rg_wi, rg_bi, rg_lambda,
             hgrn_lb_logits, s5_a_re, s5_a_im, s5_log_dt, s5_b_re, s5_b_im, s5_c_re, s5_c_im, s5_d,
             s5_glu_w, s5_glu_b, mix_gain, w_out, norm_mlp, w_up, w_down, norm_final):
    row = lambda a: a.astype(F32)[:, None, :]
    lb_c = jnp.cumsum(jax.nn.softmax(hgrn_lb_logits.astype(F32), axis=1), axis=1)
    lb = (lb_c - lb_c[:, :1]).transpose(1, 0, 2)
    lbp = jnp.stack([jnp.log(lb), jnp.log1p(-lb), 1.0 - lb], axis=2)
    wg_all = jnp.concatenate([_block_diag(rg_wr), _block_diag(rg_wi)], axis=-1).astype(BF16)
    bg_all = jnp.concatenate([rg_br, rg_bi], axis=-1).astype(F32)[:, :, None, :]
    rg = [dict(cw=conv_w[l].astype(F32), cb=conv_b[l].astype(F32)[None], wg=wg_all[l], bg=bg_all[l],
               lam=rg_lambda[l].astype(F32)[:, None, :]) for l in range(DEPTH)]
    half = jnp.arange(LANES) // B_HEAD_DIM
    hsum = (half[:, None] == half[None, :]).astype(BF16)
    gain = mix_gain.astype(F32)
    s5_all = jax.vmap(_s5_weights)(s5_a_re, s5_a_im, s5_log_dt, s5_b_re, s5_b_im, s5_c_re, s5_c_im)
    return dict(
        meta_tokens=meta_tokens, norm_mix=row(norm_mix), w_in=w_in.astype(BF16), rg=rg, lbp=lbp, hsum=hsum,
        s5=[tuple(z[l] for z in s5_all) for l in range(DEPTH)],
        s5_d=row(s5_d), glu_w=s5_glu_w.astype(BF16), glu_b=row(s5_glu_b),
        gain_a=row(gain[:, 0:A_WIDTH]), gain_b=row(gain[:, A_WIDTH:A_WIDTH + B_WIDTH]),
        gain_c=row(gain[:, A_WIDTH + B_WIDTH:]),
        w_out=w_out.astype(BF16), norm_mlp=row(norm_mlp), w_up=w_up.astype(BF16), w_down=w_down.astype(BF16),
        norm_final=norm_final.astype(F32)[None],
    )


def kernel(x_prompt, x_sample, meta_tokens, norm_mix, w_in, conv_w, conv_b, rg_wr, rg_br, rg_wi, rg_bi, rg_lambda, hgrn_lb_logits, s5_a_re, s5_a_im, s5_log_dt, s5_b_re, s5_b_im, s5_c_re, s5_c_im, s5_d, s5_glu_w, s5_glu_b, mix_gain, w_out, norm_mlp, w_up, w_down, norm_final):
    P = _prepare(meta_tokens, norm_mix, w_in, conv_w, conv_b, rg_wr, rg_br, rg_wi, rg_bi, rg_lambda,
                 hgrn_lb_logits, s5_a_re, s5_a_im, s5_log_dt, s5_b_re, s5_b_im, s5_c_re, s5_c_im, s5_d,
                 s5_glu_w, s5_glu_b, mix_gain, w_out, norm_mlp, w_up, w_down, norm_final)
    return (_trunk(x_prompt, P), _trunk(x_sample, P))
```
